```python
import jax, jax.numpy as jnp
from jax import lax
import numpy as np

D_MODEL = 1024
BATCH = 32
SEQ = 2048
DEPTH = 4

BRANCH_WIDTH = D_MODEL // 2
N_BRANCH = 3
GM_CHUNK = 128
GM_GROUPS = 4
GM_GROUP_WIDTH = BRANCH_WIDTH // GM_GROUPS
SB_HEAD_DIM = 64
SB_HEADS = BRANCH_WIDTH // SB_HEAD_DIM
SB_QBLOCK = 128
POOL_WINDOWS = (2, 4, 8, 16)
POOL_GROUPS = len(POOL_WINDOWS)
POOL_GROUP_WIDTH = BRANCH_WIDTH // POOL_GROUPS
D_FF = -(-8 * D_MODEL // (3 * 256)) * 256
N_MOD = 6
EPS = 1e-6
IN_SIZES = (BRANCH_WIDTH,) * 6 + (N_BRANCH * D_MODEL,)
IN_SPLITS = tuple(int(s) for s in np.cumsum(IN_SIZES)[:-1])
IN_COLS = int(sum(IN_SIZES))

kernel_name = "hybrid_gmlp_stickbreak_pool_adaln"


def rmsnorm(x, g):
    xf = x.astype(jnp.float32)
    xf = xf * lax.rsqrt(jnp.mean(xf * xf, axis=-1, keepdims=True) + EPS)
    return xf.astype(x.dtype) * g


def layernorm(x, g, b):
    xf = x.astype(jnp.float32)
    mu = jnp.mean(xf, axis=-1, keepdims=True)
    xc = xf - mu
    xf = xc * lax.rsqrt(jnp.mean(xc * xc, axis=-1, keepdims=True) + EPS)
    return xf.astype(x.dtype) * g + b


def gmlp_mixer(u, v, ln_g, ln_b, w_s, b_s):
    B, S, _ = v.shape
    v = layernorm(v, ln_g, ln_b)
    vc = v.reshape(B, S // GM_CHUNK, GM_CHUNK, GM_GROUPS, GM_GROUP_WIDTH)
    causal = jnp.tril(jnp.ones((GM_CHUNK, GM_CHUNK), dtype=bool))
    w = jnp.where(causal[None], w_s, 0.0)
    s = jnp.einsum('gts,bnsgc->bntgc', w, vc) + b_s.T[:, :, None]
    return u * s.reshape(B, S, BRANCH_WIDTH)


def stick_breaking_attention(q, k, v):
    B, S, _ = q.shape
    to_heads = lambda a: a.reshape(B, S, SB_HEADS, SB_HEAD_DIM).transpose(0, 2, 1, 3)
    q, k, v = to_heads(q), to_heads(k), to_heads(v)
    scale = SB_HEAD_DIM ** -0.5
    outs = []
    for i in range(S // SB_QBLOCK):
        start, end = i * SB_QBLOCK, (i + 1) * SB_QBLOCK
        qb = q[:, :, start:end].astype(jnp.float32)
        kb = k[:, :, :end].astype(jnp.float32)
        z = jnp.einsum('bhqd,bhkd->bhqk', qb, kb) * scale
        t_pos = start + jnp.arange(SB_QBLOCK)[:, None]
        s_pos = jnp.arange(end)[None, :]
        mask = s_pos < t_pos
        log_beta = jax.nn.log_sigmoid(z)
        log_one_minus = jnp.where(mask, log_beta - z, 0.0)
        suffix = lax.cumsum(log_one_minus, axis=3, reverse=True) - log_one_minus
        a = jnp.where(mask, jnp.exp(log_beta + suffix), 0.0)
        outs.append(jnp.einsum('bhqk,bhkd->bhqd', a.astype(v.dtype), v[:, :, :end]))
    o = jnp.concatenate(outs, axis=2)
    return o.transpose(0, 2, 1, 3).reshape(B, S, BRANCH_WIDTH)


def pool_mixer(xp, w_pool, pool_scale):
    B, S, _ = xp.shape
    xg = xp.astype(jnp.float32).reshape(B, S, POOL_GROUPS, POOL_GROUP_WIDTH)
    prefix = jnp.cumsum(xg, axis=1)
    pos = jnp.arange(S)
    diffs = []
    for g, w in enumerate(POOL_WINDOWS):
        pg = prefix[:, :, g]
        lagged = jnp.pad(pg[:, :S - w], ((0, 0), (w, 0), (0, 0)))
        count = jnp.minimum(pos + 1, w).astype(jnp.float32)[None, :, None]
        diffs.append((pg - lagged) / count - xg[:, :, g])
    d = jnp.stack(diffs, axis=2).astype(xp.dtype)
    y = jnp.einsum('bsgc,gcd->bsgd', d, w_pool)
    return y.reshape(B, S, BRANCH_WIDTH) * pool_scale


def setup_inputs(seed: int = 0) -> dict:
    key = jax.random.key(seed)
    ks = jax.random.split(key, 20)
    f32 = jnp.float32
    nrm = lambda k, shape, s: jax.random.normal(k, shape, f32) * s
    L, D, W = DEPTH, D_MODEL, BRANCH_WIDTH
    return {
        "x": nrm(ks[0], (BATCH, SEQ, D), 1.0),
        "c": nrm(ks[1], (BATCH, D), 1.0),
        "rms_g1": 1.0 + nrm(ks[2], (L, D), 0.02),
        "rms_g2": 1.0 + nrm(ks[3], (L, D), 0.02),
        "w_ada": nrm(ks[4], (L, D, N_MOD * D), 0.5 * D ** -0.5),
        "b_ada": nrm(ks[5], (L, N_MOD * D), 0.02),
        "w_in": nrm(ks[6], (L, D, IN_COLS), D ** -0.5),
        "gm_ln_g": 1.0 + nrm(ks[7], (L, W), 0.02),
        "gm_ln_b": nrm(ks[8], (L, W), 0.02),
        "gm_w_spatial": nrm(ks[9], (L, GM_GROUPS, GM_CHUNK, GM_CHUNK), GM_CHUNK ** -0.5),
        "gm_b_spatial": 1.0 + nrm(ks[10], (L, GM_GROUPS, GM_CHUNK), 0.02),
        "pool_w": nrm(ks[11], (L, POOL_GROUPS, POOL_GROUP_WIDTH, POOL_GROUP_WIDTH), POOL_GROUP_WIDTH ** -0.5),
        "pool_scale": 1.0 + nrm(ks[12], (L, W), 0.02),
        "w_branch": nrm(ks[13], (L, N_BRANCH, W, D), W ** -0.5),
        "w_out": nrm(ks[14], (L, D, D), D ** -0.5),
        "w_ffn_in": nrm(ks[15], (L, D, 2 * D_FF), D ** -0.5),
        "w_ffn_out": nrm(ks[16], (L, D_FF, D), D_FF ** -0.5),
        "final_g": 1.0 + nrm(ks[17], (D,), 0.02),
    }


def reference(x, c, rms_g1, rms_g2, w_ada, b_ada, w_in, gm_ln_g, gm_ln_b, gm_w_spatial, gm_b_spatial,
              pool_w, pool_scale, w_branch, w_out, w_ffn_in, w_ffn_out, final_g):
    B, S, D = x.shape
    c_act = jax.nn.silu(c)
    for l in range(DEPTH):
        mod = c_act @ w_ada[l] + b_ada[l]
        sh1, sc1, gt1, sh2, sc2, gt2 = [m[:, None, :] for m in jnp.split(mod, N_MOD, axis=-1)]

        h = rmsnorm(x, rms_g1[l]) * (1.0 + sc1) + sh1
        proj = h @ w_in[l]
        gm_u, gm_v, sb_q, sb_k, sb_v, pool_in, gate_logits = jnp.split(proj, IN_SPLITS, axis=-1)
        branches = (
            gmlp_mixer(jax.nn.gelu(gm_u), jax.nn.gelu(gm_v), gm_ln_g[l], gm_ln_b[l],
                       gm_w_spatial[l], gm_b_spatial[l]),
            stick_breaking_attention(sb_q, sb_k, sb_v),
            pool_mixer(pool_in, pool_w[l], pool_scale[l]),
        )
        gates = jax.nn.sigmoid(gate_logits.reshape(B, S, N_BRANCH, D))
        merged = sum(gates[:, :, n] * (branches[n] @ w_branch[l, n]) for n in range(N_BRANCH))
        x = x + gt1 * (merged @ w_out[l])

        h2 = rmsnorm(x, rms_g2[l]) * (1.0 + sc2) + sh2
        f_gate, f_up = jnp.split(h2 @ w_ffn_in[l], 2, axis=-1)
        x = x + gt2 * ((jax.nn.silu(f_gate) * f_up) @ w_ffn_out[l])
    return rmsnorm(x, final_g)
```

```python
import functools
import math

import jax
import jax.numpy as jnp
from jax import lax
from jax.experimental import pallas as pl
from jax.experimental.pallas import tpu as pltpu

F32 = jnp.float32
BF16 = jnp.bfloat16

D_MODEL = 1024
DEPTH = 4
BRANCH_WIDTH = D_MODEL // 2
N_BRANCH = 3
GM_CHUNK = 128
GM_GROUPS = 4
GM_GROUP_WIDTH = BRANCH_WIDTH // GM_GROUPS
SB_HEAD_DIM = 64
SB_HEADS = BRANCH_WIDTH // SB_HEAD_DIM
POOL_WINDOWS = (2, 4, 8, 16)
POOL_GROUP_WIDTH = BRANCH_WIDTH // len(POOL_WINDOWS)
D_FF = -(-8 * D_MODEL // (3 * 256)) * 256
N_MOD = 6
EPS = 1e-6
IN_COLS = 6 * BRANCH_WIDTH + N_BRANCH * D_MODEL

LANES = 128
HEAD_PAIR = 2 * SB_HEAD_DIM
N_PAIRS = SB_HEADS // 2
POOL_HALO = 16
LOG2E = 1.4426950408889634
VMEM_LIMIT = 56 * 1024 * 1024

TM_PROJ = 512
TQ = 256
TK = 256
FF_CHUNK = 256


def _resident(shape):
    nd = len(shape)
    return pl.BlockSpec(shape, lambda *_: (0,) * nd, pipeline_mode=pl.Buffered(1))


def _sigmoid(x):
    return 1.0 / (1.0 + jnp.exp(-x))


def _gelu_tanh(x):
    return 0.5 * x * (1.0 + jnp.tanh(0.7978845608028654 * (x + 0.044715 * (x * x * x))))


def _rms_modulate(x, g, sc, sh):
    ms = jnp.mean(x * x, axis=-1, keepdims=True)
    return (x * lax.rsqrt(ms + EPS)) * g * (1.0 + sc) + sh


def _ada_kernel(c_ref, w_ref, b_ref, o_ref):
    c = c_ref[...]
    c_act = c * _sigmoid(c)
    o_ref[0] = jnp.dot(c_act.astype(BF16), w_ref[0].astype(BF16), preferred_element_type=F32) + b_ref[0]


def _ada_call(c, w_ada, b_ada):
    batch = c.shape[0]
    n_out = N_MOD * D_MODEL
    tn = n_out // 4
    return pl.pallas_call(
        _ada_kernel,
        grid=(DEPTH, n_out // tn),
        in_specs=[
            pl.BlockSpec((batch, D_MODEL), lambda l, n: (0, 0)),
            pl.BlockSpec((1, D_MODEL, tn), lambda l, n: (l, 0, n)),
            pl.BlockSpec((1, 1, tn), lambda l, n: (l, 0, n)),
        ],
        out_specs=pl.BlockSpec((1, batch, tn), lambda l, n: (l, 0, n)),
        out_shape=jax.ShapeDtypeStruct((DEPTH, batch, n_out), F32),
        compiler_params=pltpu.CompilerParams(
            dimension_semantics=("arbitrary", "arbitrary"), vmem_limit_bytes=VMEM_LIMIT),
        name="ada_mod",
    )(c, w_ada, b_ada.reshape(DEPTH, 1, n_out))


def _inproj_kernel(x_ref, g_ref, sc_ref, sh_ref, w_ref, lng_ref, lnb_ref,
                   u_ref, vln_ref, q0_ref, q1_ref, k_ref, v_ref, p_ref, gate_ref):
    h = _rms_modulate(x_ref[...], g_ref[...], sc_ref[0], sh_ref[0]).astype(BF16)
    bw = BRANCH_WIDTH

    def proj(col0, ncols):
        return jnp.dot(h, w_ref[:, col0:col0 + ncols], preferred_element_type=F32)

    u_ref[...] = _gelu_tanh(proj(0, bw)).astype(BF16)

    gv = _gelu_tanh(proj(bw, bw))
    mu = jnp.mean(gv, axis=-1, keepdims=True)
    gc = gv - mu
    var = jnp.mean(gc * gc, axis=-1, keepdims=True)
    vln_ref[...] = ((gc * lax.rsqrt(var + EPS)) * lng_ref[...] + lnb_ref[...]).astype(BF16)

    q = proj(2 * bw, bw) * (SB_HEAD_DIM ** -0.5 * LOG2E)
    lane = lax.broadcasted_iota(jnp.int32, q.shape, 1)
    first_of_pair = (lane & SB_HEAD_DIM) == 0
    q0_ref[...] = jnp.where(first_of_pair, q, 0.0).astype(BF16)
    q1_ref[...] = jnp.where(first_of_pair, 0.0, q).astype(BF16)

    k_ref[...] = proj(3 * bw, bw).astype(BF16)
    v_ref[...] = proj(4 * bw, bw).astype(BF16)
    p_ref[...] = proj(5 * bw, bw).astype(BF16)
    for n in range(N_BRANCH):
        gate_ref[:, n * D_MODEL:(n + 1) * D_MODEL] = _sigmoid(
            proj(6 * bw + n * D_MODEL, D_MODEL)).astype(BF16)


def _inproj_call(x2d, rms_g, sc, sh, w_in, ln_g, ln_b, seq):
    m = x2d.shape[0]
    tm = TM_PROJ
    steps_per_seq = seq // tm
    row = lambda i: (i, 0)
    mod = lambda i: (i // steps_per_seq, 0, 0)
    piece = jax.ShapeDtypeStruct((m, BRANCH_WIDTH), BF16)
    return pl.pallas_call(
        _inproj_kernel,
        grid=(m // tm,),
        in_specs=[
            pl.BlockSpec((tm, D_MODEL), row),
            _resident((1, D_MODEL)),
            pl.BlockSpec((1, 1, D_MODEL), mod),
            pl.BlockSpec((1, 1, D_MODEL), mod),
            _resident((D_MODEL, IN_COLS)),
            _resident((1, BRANCH_WIDTH)),
            _resident((1, BRANCH_WIDTH)),
        ],
        out_specs=[pl.BlockSpec((tm, BRANCH_WIDTH), row)] * 7
        + [pl.BlockSpec((tm, N_BRANCH * D_MODEL), row)],
        out_shape=[piece] * 7 + [jax.ShapeDtypeStruct((m, N_BRANCH * D_MODEL), BF16)],
        compiler_params=pltpu.CompilerParams(
            dimension_semantics=("arbitrary",), vmem_limit_bytes=VMEM_LIMIT),
        name="in_proj",
    )(x2d, rms_g, sc, sh, w_in, ln_g, ln_b)


def _mix_kernel(x_ref, u_ref, vln_ref, q0_ref, q1_ref, k_ref, v_ref, p_ref, halo_ref, gate_ref, gt_ref,
                ws_ref, bs_ref, pw_ref, ps_ref, wbr_ref, wout_ref, tri_ref,
                o_ref, acc_ref, run_ref, pool_scr, br_scr):
    i = pl.program_id(1)
    gw = GM_GROUP_WIDTH

    row_id = lax.broadcasted_iota(jnp.int32, (TQ, TK), 0)
    col_id = lax.broadcasted_iota(jnp.int32, (TQ, TK), 1)
    strictly_causal = col_id < row_id
    first_of_pair = lax.broadcasted_iota(jnp.int32, (TQ, HEAD_PAIR), 1) < SB_HEAD_DIM

    def key_block(j, diagonal):
        rows = pl.ds(pl.multiple_of(j * TK, TK), TK)
        for p in range(N_PAIRS):
            cols = slice(p * HEAD_PAIR, (p + 1) * HEAD_PAIR)
            kp = k_ref[0, rows, cols]
            vp = v_ref[0, rows, cols]
            av = []
            for hh, q_ref in enumerate((q0_ref, q1_ref)):
                head = 2 * p + hh
                z = lax.dot_general(q_ref[0, :, cols], kp, (((1,), (1,)), ((), ())),
                                    preferred_element_type=F32)
                lse = jnp.log(1.0 + jnp.exp2(-jnp.abs(z))) * LOG2E
                log_beta = jnp.minimum(z, 0.0) - lse
                log_rest = log_beta - z
                if diagonal:
                    log_rest = jnp.where(strictly_causal, log_rest, 0.0)
                hi = log_rest.astype(BF16)
                lo = (log_rest - hi.astype(F32)).astype(BF16)
                suffix = jnp.dot(jnp.concatenate([hi, lo], axis=1), tri_ref[...],
                                 preferred_element_type=F32)
                total = log_beta + suffix
                if not diagonal:
                    total = total + run_ref[head]
                a = jnp.exp2(total)
                if diagonal:
                    a = jnp.where(strictly_causal, a, 0.0)
                av.append(jnp.dot(a.astype(BF16), vp, preferred_element_type=F32))
                block_sum = jnp.sum(log_rest, axis=1, keepdims=True)
                run_ref[head] = block_sum if diagonal else run_ref[head] + block_sum
            both = jnp.where(first_of_pair, av[0], av[1])
            acc_ref[p] = both if diagonal else acc_ref[p] + both

    key_block(i, True)

    def earlier_block(t, carry):
        key_block(i - 1 - t, False)
        return carry

    lax.fori_loop(0, i, earlier_block, 0)
    for p in range(N_PAIRS):
        br_scr[1, :, p * HEAD_PAIR:(p + 1) * HEAD_PAIR] = acc_ref[p].astype(BF16)

    tr = lax.broadcasted_iota(jnp.int32, (GM_CHUNK, GM_CHUNK), 0)
    tc = lax.broadcasted_iota(jnp.int32, (GM_CHUNK, GM_CHUNK), 1)
    for g in range(GM_GROUPS):
        cols = slice(g * gw, (g + 1) * gw)
        wg = jnp.where(tc <= tr, ws_ref[g], 0.0).astype(BF16)
        for c in range(TQ // GM_CHUNK):
            rows = slice(c * GM_CHUNK, (c + 1) * GM_CHUNK)
            s = jnp.dot(wg, vln_ref[0, rows, cols], preferred_element_type=F32) + bs_ref[:, cols]
            br_scr[0, rows, cols] = (u_ref[0, rows, cols].astype(F32) * s).astype(BF16)

    pool_scr[0:POOL_HALO, :] = jnp.where(i > 0, halo_ref[0].astype(F32), 0.0)
    pool_scr[POOL_HALO:POOL_HALO + TQ, :] = p_ref[0].astype(F32)
    pos = i * TQ + lax.broadcasted_iota(jnp.int32, (TQ, 1), 0)
    for g, w in enumerate(POOL_WINDOWS):
        cols = slice(g * POOL_GROUP_WIDTH, (g + 1) * POOL_GROUP_WIDTH)
        cur = pool_scr[POOL_HALO:POOL_HALO + TQ, cols]
        win = cur
        for back in range(1, w):
            win = win + pool_scr[POOL_HALO - back:POOL_HALO - back + TQ, cols]
        count = jnp.minimum(pos + 1, w).astype(F32)
        d = win / count - cur
        y = jnp.dot(d.astype(BF16), pw_ref[g], preferred_element_type=F32) * ps_ref[:, cols]
        br_scr[2, :, cols] = y.astype(BF16)

    merged = None
    for n in range(N_BRANCH):
        t = jnp.dot(br_scr[n], wbr_ref[n], preferred_element_type=F32)
        t = t * gate_ref[0, :, n * D_MODEL:(n + 1) * D_MODEL].astype(F32)
        merged = t if merged is None else merged + t
    y = jnp.dot(merged.astype(BF16), wout_ref[...], preferred_element_type=F32)
    o_ref[0] = x_ref[0] + gt_ref[0] * y


def _mix_call(x, u, vln, q0, q1, k, v, p, gates, gt, ws, bs_full, pw, ps, wbr, wout, tri):
    batch, seq, _ = x.shape
    bw = BRANCH_WIDTH
    rows = lambda b, i: (b, i, 0)
    whole = lambda b, i: (b, 0, 0)
    halo = lambda b, i: (b, jnp.maximum(i * (TQ // POOL_HALO) - 1, 0), 0)
    piece = pl.BlockSpec((1, TQ, bw), rows)
    return pl.pallas_call(
        _mix_kernel,
        grid=(batch, seq // TQ),
        in_specs=[
            pl.BlockSpec((1, TQ, D_MODEL), rows),
            piece, piece, piece, piece,
            pl.BlockSpec((1, seq, bw), whole),
            pl.BlockSpec((1, seq, bw), whole),
            piece,
            pl.BlockSpec((1, POOL_HALO, bw), halo),
            pl.BlockSpec((1, TQ, N_BRANCH * D_MODEL), rows),
            pl.BlockSpec((1, 1, D_MODEL), whole),
            _resident((GM_GROUPS, GM_CHUNK, GM_CHUNK)),
            _resident((GM_CHUNK, bw)),
            _resident((len(POOL_WINDOWS), POOL_GROUP_WIDTH, POOL_GROUP_WIDTH)),
            _resident((1, bw)),
            _resident((N_BRANCH, bw, D_MODEL)),
            _resident((D_MODEL, D_MODEL)),
            _resident((2 * TK, TK)),
        ],
        out_specs=pl.BlockSpec((1, TQ, D_MODEL), rows),
        out_shape=jax.ShapeDtypeStruct(x.shape, F32),
        scratch_shapes=[
            pltpu.VMEM((N_PAIRS, TQ, HEAD_PAIR), F32),
            pltpu.VMEM((SB_HEADS, TQ, 1), F32),
            pltpu.VMEM((POOL_HALO + TQ, bw), F32),
            pltpu.VMEM((N_BRANCH, TQ, bw), BF16),
        ],
        compiler_params=pltpu.CompilerParams(
            dimension_semantics=("arbitrary", "arbitrary"), vmem_limit_bytes=VMEM_LIMIT),
        name="token_mix",
    )(x, u, vln, q0, q1, k, v, p, p, gates, gt, ws, bs_full, pw, ps, wbr, wout, tri)


def _ffn_kernel(x_ref, g_ref, sc_ref, sh_ref, gt_ref, win_ref, wout_ref, fg_ref, o_ref, hid_scr, *, final):
    x = x_ref[...]
    h = _rms_modulate(x, g_ref[...], sc_ref[0], sh_ref[0]).astype(BF16)
    for c in range(D_FF // FF_CHUNK):
        c0 = c * FF_CHUNK
        f_gate = jnp.dot(h, win_ref[:, c0:c0 + FF_CHUNK], preferred_element_type=F32)
        f_up = jnp.dot(h, win_ref[:, D_FF + c0:D_FF + c0 + FF_CHUNK], preferred_element_type=F32)
        hid_scr[:, c0:c0 + FF_CHUNK] = (f_gate * _sigmoid(f_gate) * f_up).astype(BF16)
    y = jnp.dot(hid_scr[...], wout_ref[...], preferred_element_type=F32)
    xn = x + gt_ref[0] * y
    if final:
        ms = jnp.mean(xn * xn, axis=-1, keepdims=True)
        xn = (xn * lax.rsqrt(ms + EPS)) * fg_ref[...]
    o_ref[...] = xn


def _ffn_call(x2d, rms_g, sc, sh, gt, w_in, w_out, final_g, seq, final):
    m = x2d.shape[0]
    tm = TM_PROJ
    steps_per_seq = seq // tm
    row = lambda i: (i, 0)
    mod = lambda i: (i // steps_per_seq, 0, 0)
    return pl.pallas_call(
        functools.partial(_ffn_kernel, final=final),
        grid=(m // tm,),
        in_specs=[
            pl.BlockSpec((tm, D_MODEL), row),
            _resident((1, D_MODEL)),
            pl.BlockSpec((1, 1, D_MODEL), mod),
            pl.BlockSpec((1, 1, D_MODEL), mod),
            pl.BlockSpec((1, 1, D_MODEL), mod),
            _resident((D_MODEL, 2 * D_FF)),
            _resident((D_FF, D_MODEL)),
            _resident((1, D_MODEL)),
        ],
        out_specs=pl.BlockSpec((tm, D_MODEL), row),
        out_shape=jax.ShapeDtypeStruct(x2d.shape, F32),
        scratch_shapes=[pltpu.VMEM((tm, D_FF), BF16)],
        compiler_params=pltpu.CompilerParams(
            dimension_semantics=("arbitrary",), vmem_limit_bytes=VMEM_LIMIT),
        name="ffn_final" if final else "ffn",
    )(x2d, rms_g, sc, sh, gt, w_in, w_out, final_g)


def _suffix_matrix():
    j = lax.broadcasted_iota(jnp.int32, (TK, TK), 0)
    s = lax.broadcasted_iota(jnp.int32, (TK, TK), 1)
    u = (j > s).astype(BF16)
    return jnp.concatenate([u, u], axis=0)


def kernel(x, c, rms_g1, rms_g2, w_ada, b_ada, w_in, gm_ln_g, gm_ln_b, gm_w_spatial, gm_b_spatial,
           pool_w, pool_scale, w_branch, w_out, w_ffn_in, w_ffn_out, final_g):
    batch, seq, d = x.shape
    assert d == D_MODEL and seq % TM_PROJ == 0 and seq % TQ == 0 and TQ == TK and TQ % GM_CHUNK == 0
    m = batch * seq

    mod = _ada_call(c, w_ada, b_ada)
    mod = mod.reshape(DEPTH, batch, N_MOD, 1, D_MODEL)
    tri = _suffix_matrix()
    final_row = final_g.reshape(1, D_MODEL)

    for l in range(DEPTH):
        sh1, sc1, gt1, sh2, sc2, gt2 = [mod[l, :, n] for n in range(N_MOD)]
        u, vln, q0, q1, k, v, p, gates = _inproj_call(
            x.reshape(m, D_MODEL), rms_g1[l].reshape(1, D_MODEL), sc1, sh1, w_in[l].astype(BF16),
            gm_ln_g[l].reshape(1, BRANCH_WIDTH), gm_ln_b[l].reshape(1, BRANCH_WIDTH), seq)
        to_seq = lambda a: a.reshape(batch, seq, a.shape[-1])
        bs_full = jnp.repeat(gm_b_spatial[l].T, GM_GROUP_WIDTH, axis=1)
        x = _mix_call(
            x, to_seq(u), to_seq(vln), to_seq(q0), to_seq(q1), to_seq(k), to_seq(v), to_seq(p),
            to_seq(gates), gt1, gm_w_spatial[l], bs_full, pool_w[l].astype(BF16),
            pool_scale[l].reshape(1, BRANCH_WIDTH), w_branch[l].astype(BF16), w_out[l].astype(BF16), tri)
        x = _ffn_call(
            x.reshape(m, D_MODEL), rms_g2[l].reshape(1, D_MODEL), sc2, sh2, gt2,
            w_ffn_in[l].astype(BF16), w_ffn_out[l].astype(BF16), final_row, seq,
            final=(l == DEPTH - 1)).reshape(batch, seq, D_MODEL)
    return x
```

```python
import functools
import math

import jax
import jax.numpy as jnp
from jax import lax
from jax.experimental import pallas as pl
from jax.experimental.pallas import tpu as pltpu

F32 = jnp.float32
BF16 = jnp.bfloat16

D_MODEL = 1024
DEPTH = 4
BRANCH_WIDTH = D_MODEL // 2
N_BRANCH = 3
GM_CHUNK = 128
GM_GROUPS = 4
GM_GROUP_WIDTH = BRANCH_WIDTH // GM_GROUPS
SB_HEAD_DIM = 64
SB_HEADS = BRANCH_WIDTH // SB_HEAD_DIM
POOL_WINDOWS = (2, 4, 8, 16)
POOL_GROUP_WIDTH = BRANCH_WIDTH // len(POOL_WINDOWS)
D_FF = -(-8 * D_MODEL // (3 * 256)) * 256
N_MOD = 6
EPS = 1e-6
IN_COLS = 6 * BRANCH_WIDTH + N_BRANCH * D_MODEL

LANES = 128
HEAD_PAIR = 2 * SB_HEAD_DIM
N_PAIRS = SB_HEADS // 2
POOL_HALO = 16
LOG2E = 1.4426950408889634
VMEM_LIMIT = 56 * 1024 * 1024

TM_PROJ = 512
TQ = 256
TK = 256
FF_CHUNK = 256


def _resident(shape):
    nd = len(shape)
    return pl.BlockSpec(shape, lambda *_: (0,) * nd, pipeline_mode=pl.Buffered(1))


def _sigmoid(x):
    return 1.0 / (1.0 + jnp.exp(-x))


def _gelu_tanh(x):
    return 0.5 * x * (1.0 + jnp.tanh(0.7978845608028654 * (x + 0.044715 * (x * x * x))))


def _rms_modulate(x, g, sc, sh):
    ms = jnp.mean(x * x, axis=-1, keepdims=True)
    return (x * lax.rsqrt(ms + EPS)) * g * (1.0 + sc) + sh


def _ada_kernel(c_ref, w_ref, b_ref, o_ref):
    c = c_ref[...]
    c_act = c * _sigmoid(c)
    o_ref[0] = jnp.dot(c_act.astype(BF16), w_ref[0].astype(BF16), preferred_element_type=F32) + b_ref[0]


def _ada_call(c, w_ada, b_ada):
    batch = c.shape[0]
    n_out = N_MOD * D_MODEL
    tn = n_out // 4
    return pl.pallas_call(
        _ada_kernel,
        grid=(DEPTH, n_out // tn),
        in_specs=[
            pl.BlockSpec((batch, D_MODEL), lambda l, n: (0, 0)),
            pl.BlockSpec((1, D_MODEL, tn), lambda l, n: (l, 0, n)),
            pl.BlockSpec((1, 1, tn), lambda l, n: (l, 0, n)),
        ],
        out_specs=pl.BlockSpec((1, batch, tn), lambda l, n: (l, 0, n)),
        out_shape=jax.ShapeDtypeStruct((DEPTH, batch, n_out), F32),
        compiler_params=pltpu.CompilerParams(
            dimension_semantics=("arbitrary", "arbitrary"), vmem_limit_bytes=VMEM_LIMIT),
        name="ada_mod",
    )(c, w_ada, b_ada.reshape(DEPTH, 1, n_out))


def _inproj_kernel(x_ref, g_ref, sc_ref, sh_ref, w_ref, lng_ref, lnb_ref,
                   u_ref, vln_ref, q0_ref, q1_ref, k_ref, v_ref, p_ref, gate_ref):
    h = _rms_modulate(x_ref[...], g_ref[...], sc_ref[0], sh_ref[0]).astype(BF16)
    bw = BRANCH_WIDTH

    def proj(col0, ncols):
        return jnp.dot(h, w_ref[:, col0:col0 + ncols], preferred_element_type=F32)

    u_ref[...] = _gelu_tanh(proj(0, bw)).astype(BF16)

    gv = _gelu_tanh(proj(bw, bw))
    mu = jnp.mean(gv, axis=-1, keepdims=True)
    gc = gv - mu
    var = jnp.mean(gc * gc, axis=-1, keepdims=True)
    vln_ref[...] = ((gc * lax.rsqrt(var + EPS)) * lng_ref[...] + lnb_ref[...]).astype(BF16)

    q = proj(2 * bw, bw) * (SB_HEAD_DIM ** -0.5 * LOG2E)
    lane = lax.broadcasted_iota(jnp.int32, q.shape, 1)
    first_of_pair = (lane & SB_HEAD_DIM) == 0
    q0_ref[...] = jnp.where(first_of_pair, q, 0.0).astype(BF16)
    q1_ref[...] = jnp.where(first_of_pair, 0.0, q).astype(BF16)

    k_ref[...] = proj(3 * bw, bw).astype(BF16)
    v_ref[...] = proj(4 * bw, bw).astype(BF16)
    p_ref[...] = proj(5 * bw, bw).astype(BF16)
    for n in range(N_BRANCH):
        gate_ref[:, n * D_MODEL:(n + 1) * D_MODEL] = _sigmoid(
            proj(6 * bw + n * D_MODEL, D_MODEL)).astype(BF16)


def _inproj_call(x2d, rms_g, sc, sh, w_in, ln_g, ln_b, seq):
    m = x2d.shape[0]
    tm = TM_PROJ
    steps_per_seq = seq // tm
    row = lambda i: (i, 0)
    mod = lambda i: (i // steps_per_seq, 0, 0)
    piece = jax.ShapeDtypeStruct((m, BRANCH_WIDTH), BF16)
    return pl.pallas_call(
        _inproj_kernel,
        grid=(m // tm,),
        in_specs=[
            pl.BlockSpec((tm, D_MODEL), row),
            _resident((1, D_MODEL)),
            pl.BlockSpec((1, 1, D_MODEL), mod),
            pl.BlockSpec((1, 1, D_MODEL), mod),
            _resident((D_MODEL, IN_COLS)),
            _resident((1, BRANCH_WIDTH)),
            _resident((1, BRANCH_WIDTH)),
        ],
        out_specs=[pl.BlockSpec((tm, BRANCH_WIDTH), row)] * 7
        + [pl.BlockSpec((tm, N_BRANCH * D_MODEL), row)],
        out_shape=[piece] * 7 + [jax.ShapeDtypeStruct((m, N_BRANCH * D_MODEL), BF16)],
        compiler_params=pltpu.CompilerParams(
            dimension_semantics=("arbitrary",), vmem_limit_bytes=VMEM_LIMIT),
        name="in_proj",
    )(x2d, rms_g, sc, sh, w_in, ln_g, ln_b)


def _mix_kernel(x_ref, u_ref, vln_ref, q0_ref, q1_ref, k_ref, v_ref, p_ref, halo_ref, gate_ref, gt_ref,
                ws_ref, bs_ref, pw_ref, ps_ref, wbr_ref, wout_ref, tri_ref,
                o_ref, acc_ref, run_ref, pool_scr, br_scr):
    i = pl.program_id(1)
    gw = GM_GROUP_WIDTH

    row_id = lax.broadcasted_iota(jnp.int32, (TQ, TK), 0)
    col_id = lax.broadcasted_iota(jnp.int32, (TQ, TK), 1)
    strictly_causal = col_id < row_id
    first_of_pair = lax.broadcasted_iota(jnp.int32, (TQ, HEAD_PAIR), 1) < SB_HEAD_DIM

    q_refs = (q0_ref, q1_ref)

    def key_block(j, diagonal):
        rows = pl.ds(pl.multiple_of(j * TK, TK), TK)
        pair_cols = lambda head: slice((head // 2) * HEAD_PAIR, (head // 2 + 1) * HEAD_PAIR)
        log_beta, split, weights, av = {}, {}, {}, {}

        def scores(head):
            z = lax.dot_general(q_refs[head % 2][0, :, pair_cols(head)], k_ref[0, rows, pair_cols(head)],
                                (((1,), (1,)), ((), ())), preferred_element_type=F32)
            lse = jnp.log(1.0 + jnp.exp2(-jnp.abs(z))) * LOG2E
            log_beta[head] = jnp.minimum(z, 0.0) - lse
            log_rest = log_beta[head] - z
            if diagonal:
                log_rest = jnp.where(strictly_causal, log_rest, 0.0)
            hi = log_rest.astype(BF16)
            lo = (log_rest - hi.astype(F32)).astype(BF16)
            split[head] = jnp.concatenate([hi, lo], axis=1)
            block_sum = jnp.broadcast_to(jnp.sum(log_rest, axis=1, keepdims=True), (TQ, LANES))
            return block_sum

        def suffix_and_weights(head, block_sum):
            suffix = jnp.dot(split.pop(head), tri_ref[...], preferred_element_type=F32)
            total = log_beta.pop(head) + suffix
            if diagonal:
                run_ref[head] = block_sum
            else:
                later = run_ref[head]
                total = total + jnp.concatenate([later] * (TK // LANES), axis=1)
                run_ref[head] = later + block_sum
            a = jnp.exp2(total)
            if diagonal:
                a = jnp.where(strictly_causal, a, 0.0)
            weights[head] = a.astype(BF16)

        def weighted_values(head):
            av[head] = jnp.dot(weights.pop(head), v_ref[0, rows, pair_cols(head)],
                               preferred_element_type=F32)
            if head % 2:
                both = jnp.where(first_of_pair, av.pop(head - 1), av.pop(head))
                pair = head // 2
                acc_ref[pair] = both if diagonal else acc_ref[pair] + both

        sums = {}
        for step in range(SB_HEADS + 2):
            if step < SB_HEADS:
                sums[step] = scores(step)
            if 1 <= step <= SB_HEADS:
                suffix_and_weights(step - 1, sums.pop(step - 1))
            if step >= 2:
                weighted_values(step - 2)

    key_block(i, True)

    def earlier_block(t, carry):
        key_block(i - 1 - t, False)
        return carry

    lax.fori_loop(0, i, earlier_block, 0)
    for p in range(N_PAIRS):
        br_scr[1, :, p * HEAD_PAIR:(p + 1) * HEAD_PAIR] = acc_ref[p].astype(BF16)

    tr = lax.broadcasted_iota(jnp.int32, (GM_CHUNK, GM_CHUNK), 0)
    tc = lax.broadcasted_iota(jnp.int32, (GM_CHUNK, GM_CHUNK), 1)
    for g in range(GM_GROUPS):
        cols = slice(g * gw, (g + 1) * gw)
        wg = jnp.where(tc <= tr, ws_ref[g], 0.0).astype(BF16)
        for c in range(TQ // GM_CHUNK):
            rows = slice(c * GM_CHUNK, (c + 1) * GM_CHUNK)
            s = jnp.dot(wg, vln_ref[0, rows, cols], preferred_element_type=F32) + bs_ref[:, cols]
            br_scr[0, rows, cols] = (u_ref[0, rows, cols].astype(F32) * s).astype(BF16)

    pool_scr[0:POOL_HALO, :] = jnp.where(i > 0, halo_ref[0].astype(F32), 0.0)
    pool_scr[POOL_HALO:POOL_HALO + TQ, :] = p_ref[0].astype(F32)
    pos = i * TQ + lax.broadcasted_iota(jnp.int32, (TQ, 1), 0)
    for g, w in enumerate(POOL_WINDOWS):
        cols = slice(g * POOL_GROUP_WIDTH, (g + 1) * POOL_GROUP_WIDTH)
        cur = pool_scr[POOL_HALO:POOL_HALO + TQ, cols]
        win = cur
        for back in range(1, w):
            win = win + pool_scr[POOL_HALO - back:POOL_HALO - back + TQ, cols]
        count = jnp.minimum(pos + 1, w).astype(F32)
        d = win / count - cur
        y = jnp.dot(d.astype(BF16), pw_ref[g], preferred_element_type=F32) * ps_ref[:, cols]
        br_scr[2, :, cols] = y.astype(BF16)

    merged = None
    for n in range(N_BRANCH):
        t = jnp.dot(br_scr[n], wbr_ref[n], preferred_element_type=F32)
        t = t * gate_ref[0, :, n * D_MODEL:(n + 1) * D_MODEL].astype(F32)
        merged = t if merged is None else merged + t
    y = jnp.dot(merged.astype(BF16), wout_ref[...], preferred_element_type=F32)
    o_ref[0] = x_ref[0] + gt_ref[0] * y


def _mix_call(x, u, vln, q0, q1, k, v, p, gates, gt, ws, bs_full, pw, ps, wbr, wout, tri):
    batch, seq, _ = x.shape
    bw = BRANCH_WIDTH
    rows = lambda b, i: (b, i, 0)
    whole = lambda b, i: (b, 0, 0)
    halo = lambda b, i: (b, jnp.maximum(i * (TQ // POOL_HALO) - 1, 0), 0)
    piece = pl.BlockSpec((1, TQ, bw), rows)
    return pl.pallas_call(
        _mix_kernel,
        grid=(batch, seq // TQ),
        in_specs=[
            pl.BlockSpec((1, TQ, D_MODEL), rows),
            piece, piece, piece, piece,
            pl.BlockSpec((1, seq, bw), whole),
            pl.BlockSpec((1, seq, bw), whole),
            piece,
            pl.BlockSpec((1, POOL_HALO, bw), halo),
            pl.BlockSpec((1, TQ, N_BRANCH * D_MODEL), rows),
            pl.BlockSpec((1, 1, D_MODEL), whole),
            _resident((GM_GROUPS, GM_CHUNK, GM_CHUNK)),
            _resident((GM_CHUNK, bw)),
            _resident((len(POOL_WINDOWS), POOL_GROUP_WIDTH, POOL_GROUP_WIDTH)),
            _resident((1, bw)),
            _resident((N_BRANCH, bw, D_MODEL)),
            _resident((D_MODEL, D_MODEL)),
            _resident((2 * TK, TK)),
        ],
        out_specs=pl.BlockSpec((1, TQ, D_MODEL), rows),
        out_shape=jax.ShapeDtypeStruct(x.shape, F32),
        scratch_shapes=[
            pltpu.VMEM((N_PAIRS, TQ, HEAD_PAIR), F32),
            pltpu.VMEM((SB_HEADS, TQ, LANES), F32),
            pltpu.VMEM((POOL_HALO + TQ, bw), F32),
            pltpu.VMEM((N_BRANCH, TQ, bw), BF16),
        ],
        compiler_params=pltpu.CompilerParams(
            dimension_semantics=("arbitrary", "arbitrary"), vmem_limit_bytes=VMEM_LIMIT),
        name="token_mix",
    )(x, u, vln, q0, q1, k, v, p, p, gates, gt, ws, bs_full, pw, ps, wbr, wout, tri)


def _ffn_kernel(x_ref, g_ref, sc_ref, sh_ref, gt_ref, win_ref, wout_ref, fg_ref, o_ref, hid_scr, *, final):
    x = x_ref[...]
    h = _rms_modulate(x, g_ref[...], sc_ref[0], sh_ref[0]).astype(BF16)
    for c in range(D_FF // FF_CHUNK):
        c0 = c * FF_CHUNK
        f_gate = jnp.dot(h, win_ref[:, c0:c0 + FF_CHUNK], preferred_element_type=F32)
        f_up = jnp.dot(h, win_ref[:, D_FF + c0:D_FF + c0 + FF_CHUNK], preferred_element_type=F32)
        hid_scr[:, c0:c0 + FF_CHUNK] = (f_gate * _sigmoid(f_gate) * f_up).astype(BF16)
    y = jnp.dot(hid_scr[...], wout_ref[...], preferred_element_type=F32)
    xn = x + gt_ref[0] * y
    if final:
        ms = jnp.mean(xn * xn, axis=-1, keepdims=True)
        xn = (xn * lax.rsqrt(ms + EPS)) * fg_ref[...]
    o_ref[...] = xn


def _ffn_call(x2d, rms_g, sc, sh, gt, w_in, w_out, final_g, seq, final):
    m = x2d.shape[0]
    tm = TM_PROJ
    steps_per_seq = seq // tm
    row = lambda i: (i, 0)
    mod = lambda i: (i // steps_per_seq, 0, 0)
    return pl.pallas_call(
        functools.partial(_ffn_kernel, final=final),
        grid=(m // tm,),
        in_specs=[
            pl.BlockSpec((tm, D_MODEL), row),
            _resident((1, D_MODEL)),
            pl.BlockSpec((1, 1, D_MODEL), mod),
            pl.BlockSpec((1, 1, D_MODEL), mod),
            pl.BlockSpec((1, 1, D_MODEL), mod),
            _resident((D_MODEL, 2 * D_FF)),
            _resident((D_FF, D_MODEL)),
            _resident((1, D_MODEL)),
        ],
        out_specs=pl.BlockSpec((tm, D_MODEL), row),
        out_shape=jax.ShapeDtypeStruct(x2d.shape, F32),
        scratch_shapes=[pltpu.VMEM((tm, D_FF), BF16)],
        compiler_params=pltpu.CompilerParams(
            dimension_semantics=("arbitrary",), vmem_limit_bytes=VMEM_LIMIT),
        name="ffn_final" if final else "ffn",
    )(x2d, rms_g, sc, sh, gt, w_in, w_out, final_g)


def _suffix_matrix():
    j = lax.broadcasted_iota(jnp.int32, (TK, TK), 0)
    s = lax.broadcasted_iota(jnp.int32, (TK, TK), 1)
    u = (j > s).astype(BF16)
    return jnp.concatenate([u, u], axis=0)


def kernel(x, c, rms_g1, rms_g2, w_ada, b_ada, w_in, gm_ln_g, gm_ln_b, gm_w_spatial, gm_b_spatial,
           pool_w, pool_scale, w_branch, w_out, w_ffn_in, w_ffn_out, final_g):
    batch, seq, d = x.shape
    assert d == D_MODEL and seq % TM_PROJ == 0 and seq % TQ == 0 and TQ == TK and TQ % GM_CHUNK == 0
    m = batch * seq

    mod = _ada_call(c, w_ada, b_ada)
    mod = mod.reshape(DEPTH, batch, N_MOD, 1, D_MODEL)
    tri = _suffix_matrix()
    final_row = final_g.reshape(1, D_MODEL)

    for l in range(DEPTH):
        sh1, sc1, gt1, sh2, sc2, gt2 = [mod[l, :, n] for n in range(N_MOD)]
        u, vln, q0, q1, k, v, p, gates = _inproj_call(
            x.reshape(m, D_MODEL), rms_g1[l].reshape(1, D_MODEL), sc1, sh1, w_in[l].astype(BF16),
            gm_ln_g[l].reshape(1, BRANCH_WIDTH), gm_ln_b[l].reshape(1, BRANCH_WIDTH), seq)
        to_seq = lambda a: a.reshape(batch, seq, a.shape[-1])
        bs_full = jnp.repeat(gm_b_spatial[l].T, GM_GROUP_WIDTH, axis=1)
        x = _mix_call(
            x, to_seq(u), to_seq(vln), to_seq(q0), to_seq(q1), to_seq(k), to_seq(v), to_seq(p),
            to_seq(gates), gt1, gm_w_spatial[l], bs_full, pool_w[l].astype(BF16),
            pool_scale[l].reshape(1, BRANCH_WIDTH), w_branch[l].astype(BF16), w_out[l].astype(BF16), tri)
        x = _ffn_call(
            x.reshape(m, D_MODEL), rms_g2[l].reshape(1, D_MODEL), sc2, sh2, gt2,
            w_ffn_in[l].astype(BF16), w_ffn_out[l].astype(BF16), final_row, seq,
            final=(l == DEPTH - 1)).reshape(batch, seq, D_MODEL)
    return x
```

```python
import functools
import math

import jax
import jax.numpy as jnp
from jax import lax
from jax.experimental import pallas as pl
from jax.experimental.pallas import tpu as pltpu

F32 = jnp.float32
BF16 = jnp.bfloat16

D_MODEL = 1024
DEPTH = 4
BRANCH_WIDTH = D_MODEL // 2
N_BRANCH = 3
GM_CHUNK = 128
GM_GROUPS = 4
GM_GROUP_WIDTH = BRANCH_WIDTH // GM_GROUPS
SB_HEAD_DIM = 64
SB_HEADS = BRANCH_WIDTH // SB_HEAD_DIM
POOL_WINDOWS = (2, 4, 8, 16)
POOL_GROUP_WIDTH = BRANCH_WIDTH // len(POOL_WINDOWS)
D_FF = -(-8 * D_MODEL // (3 * 256)) * 256
N_MOD = 6
EPS = 1e-6
IN_COLS = 6 * BRANCH_WIDTH + N_BRANCH * D_MODEL

LANES = 128
MXU_COLS = 256
HEAD_PAIR = 2 * SB_HEAD_DIM
N_PAIRS = SB_HEADS // 2
POOL_HALO = 16
LOG2E = 1.4426950408889634
LOG2_F32_UNDERFLOW = -152.0
VMEM_LIMIT = 56 * 1024 * 1024

TM_PROJ = 512
TQ = 256
TK = 256
FF_CHUNK = 256


def _resident(shape):
    nd = len(shape)
    return pl.BlockSpec(shape, lambda *_: (0,) * nd, pipeline_mode=pl.Buffered(1))


def _sigmoid(x):
    return 1.0 / (1.0 + jnp.exp(-x))


def _gelu_tanh(x):
    return 0.5 * x * (1.0 + jnp.tanh(0.7978845608028654 * (x + 0.044715 * (x * x * x))))


def _rms_modulate(x, g, sc, sh):
    ms = jnp.mean(x * x, axis=-1, keepdims=True)
    return (x * lax.rsqrt(ms + EPS)) * g * (1.0 + sc) + sh


def _ada_kernel(c_ref, w_ref, b_ref, o_ref):
    c = c_ref[...]
    c_act = c * _sigmoid(c)
    o_ref[0] = jnp.dot(c_act.astype(BF16), w_ref[0].astype(BF16), preferred_element_type=F32) + b_ref[0]


def _ada_call(c, w_ada, b_ada):
    batch = c.shape[0]
    n_out = N_MOD * D_MODEL
    tn = n_out // 4
    return pl.pallas_call(
        _ada_kernel,
        grid=(DEPTH, n_out // tn),
        in_specs=[
            pl.BlockSpec((batch, D_MODEL), lambda l, n: (0, 0)),
            pl.BlockSpec((1, D_MODEL, tn), lambda l, n: (l, 0, n)),
            pl.BlockSpec((1, 1, tn), lambda l, n: (l, 0, n)),
        ],
        out_specs=pl.BlockSpec((1, batch, tn), lambda l, n: (l, 0, n)),
        out_shape=jax.ShapeDtypeStruct((DEPTH, batch, n_out), F32),
        compiler_params=pltpu.CompilerParams(
            dimension_semantics=("arbitrary", "arbitrary"), vmem_limit_bytes=VMEM_LIMIT),
        name="ada_mod",
    )(c, w_ada, b_ada.reshape(DEPTH, 1, n_out))


def _inproj_kernel(x_ref, g_ref, sc_ref, sh_ref, w_ref, lng_ref, lnb_ref,
                   u_ref, vln_ref, q0_ref, q1_ref, k_ref, v_ref, p_ref, gate_ref):
    h = _rms_modulate(x_ref[...], g_ref[...], sc_ref[0], sh_ref[0]).astype(BF16)
    bw = BRANCH_WIDTH

    nc = MXU_COLS
    halves = [slice(c * nc, (c + 1) * nc) for c in range(bw // nc)]

    def proj(col0):
        return jnp.dot(h, w_ref[:, col0:col0 + nc], preferred_element_type=F32)

    for c in range(N_BRANCH * D_MODEL // nc):
        gate_ref[:, c * nc:(c + 1) * nc] = _sigmoid(proj(6 * bw + c * nc)).astype(BF16)

    for c, cols in enumerate(halves):
        u_ref[:, cols] = _gelu_tanh(proj(c * nc)).astype(BF16)

    gv = [_gelu_tanh(proj(bw + c * nc)) for c in range(len(halves))]
    mu = sum(jnp.sum(g, axis=-1, keepdims=True) for g in gv) * (1.0 / bw)
    gc = [g - mu for g in gv]
    var = sum(jnp.sum(g * g, axis=-1, keepdims=True) for g in gc) * (1.0 / bw)
    inv = lax.rsqrt(var + EPS)
    for g, cols in zip(gc, halves):
        vln_ref[:, cols] = ((g * inv) * lng_ref[:, cols] + lnb_ref[:, cols]).astype(BF16)

    lane = lax.broadcasted_iota(jnp.int32, (x_ref.shape[0], nc), 1)
    first_of_pair = (lane & SB_HEAD_DIM) == 0
    for c, cols in enumerate(halves):
        q = proj(2 * bw + c * nc) * (SB_HEAD_DIM ** -0.5 * LOG2E)
        q0_ref[:, cols] = jnp.where(first_of_pair, q, 0.0).astype(BF16)
        q1_ref[:, cols] = jnp.where(first_of_pair, 0.0, q).astype(BF16)

    for n, ref in enumerate((k_ref, v_ref, p_ref)):
        for c, cols in enumerate(halves):
            ref[:, cols] = proj((3 + n) * bw + c * nc).astype(BF16)


def _inproj_call(x2d, rms_g, sc, sh, w_in, ln_g, ln_b, seq):
    m = x2d.shape[0]
    tm = TM_PROJ
    steps_per_seq = seq // tm
    row = lambda i: (i, 0)
    mod = lambda i: (i // steps_per_seq, 0, 0)
    piece = jax.ShapeDtypeStruct((m, BRANCH_WIDTH), BF16)
    return pl.pallas_call(
        _inproj_kernel,
        grid=(m // tm,),
        in_specs=[
            pl.BlockSpec((tm, D_MODEL), row),
            _resident((1, D_MODEL)),
            pl.BlockSpec((1, 1, D_MODEL), mod),
            pl.BlockSpec((1, 1, D_MODEL), mod),
            _resident((D_MODEL, IN_COLS)),
            _resident((1, BRANCH_WIDTH)),
            _resident((1, BRANCH_WIDTH)),
        ],
        out_specs=[pl.BlockSpec((tm, BRANCH_WIDTH), row)] * 7
        + [pl.BlockSpec((tm, N_BRANCH * D_MODEL), row)],
        out_shape=[piece] * 7 + [jax.ShapeDtypeStruct((m, N_BRANCH * D_MODEL), BF16)],
        compiler_params=pltpu.CompilerParams(
            dimension_semantics=("arbitrary",), vmem_limit_bytes=VMEM_LIMIT),
        name="in_proj",
    )(x2d, rms_g, sc, sh, w_in, ln_g, ln_b)


def _mix_kernel(x_ref, u_ref, vln_ref, q0_ref, q1_ref, k_ref, v_ref, p_ref, halo_ref, gate_ref, gt_ref,
                ws_ref, bs_ref, pw_ref, ps_ref, wbr_ref, wout_ref, tri_ref,
                o_ref, acc_ref, run_ref, pool_scr, br_scr):
    i = pl.program_id(1)
    gw = GM_GROUP_WIDTH

    row_id = lax.broadcasted_iota(jnp.int32, (TQ, TK), 0)
    col_id = lax.broadcasted_iota(jnp.int32, (TQ, TK), 1)
    strictly_causal = col_id < row_id
    first_of_pair = lax.broadcasted_iota(jnp.int32, (TQ, HEAD_PAIR), 1) < SB_HEAD_DIM

    q_refs = (q0_ref, q1_ref)

    def key_block(j, diagonal):
        rows = pl.ds(pl.multiple_of(j * TK, TK), TK)
        pair_cols = lambda head: slice((head // 2) * HEAD_PAIR, (head // 2 + 1) * HEAD_PAIR)
        log_beta, split, weights, av = {}, {}, {}, {}

        def scores(head):
            z = lax.dot_general(q_refs[head % 2][0, :, pair_cols(head)], k_ref[0, rows, pair_cols(head)],
                                (((1,), (1,)), ((), ())), preferred_element_type=F32)
            neg_abs = lax.bitcast_convert_type(
                lax.bitcast_convert_type(z, jnp.int32) | jnp.int32(-2 ** 31), F32)
            lse = jnp.log(1.0 + jnp.exp2(neg_abs)) * LOG2E
            log_beta[head] = jnp.minimum(z, 0.0) - lse
            log_rest = log_beta[head] - z
            if diagonal:
                log_rest = jnp.where(strictly_causal, log_rest, 0.0)
            split[head] = log_rest.astype(BF16)
            block_sum = jnp.broadcast_to(jnp.sum(log_rest, axis=1, keepdims=True), (TQ, LANES))
            return block_sum

        def suffix_and_weights(head, block_sum):
            suffix = jnp.dot(split.pop(head), tri_ref[...], preferred_element_type=F32)
            total = log_beta.pop(head) + suffix
            if diagonal:
                running[head] = block_sum
            else:
                later = run_ref[head]
                total = total + jnp.concatenate([later] * (TK // LANES), axis=1)
                running[head] = later + block_sum
            run_ref[head] = running[head]
            a = jnp.exp2(total)
            if diagonal:
                a = jnp.where(strictly_causal, a, 0.0)
            weights[head] = a.astype(BF16)

        def weighted_values(head):
            av[head] = jnp.dot(weights.pop(head), v_ref[0, rows, pair_cols(head)],
                               preferred_element_type=F32)
            if head % 2:
                both = jnp.where(first_of_pair, av.pop(head - 1), av.pop(head))
                pair = head // 2
                acc_ref[pair] = both if diagonal else acc_ref[pair] + both

        sums, running = {}, {}
        for step in range(SB_HEADS + 2):
            if step < SB_HEADS:
                sums[step] = scores(step)
            if 1 <= step <= SB_HEADS:
                suffix_and_weights(step - 1, sums.pop(step - 1))
            if step >= 2:
                weighted_values(step - 2)
        least_decayed = functools.reduce(jnp.maximum, [running[h] for h in range(SB_HEADS)])
        return jnp.max(least_decayed) > LOG2_F32_UNDERFLOW

    def earlier_block(carry):
        t, _ = carry
        return t + 1, key_block(i - 1 - t, False)

    lax.while_loop(lambda carry: (carry[0] < i) & carry[1], earlier_block,
                   (jnp.int32(0), key_block(i, True)))
    for p in range(N_PAIRS):
        br_scr[1, :, p * HEAD_PAIR:(p + 1) * HEAD_PAIR] = acc_ref[p].astype(BF16)

    tr = lax.broadcasted_iota(jnp.int32, (GM_CHUNK, GM_CHUNK), 0)
    tc = lax.broadcasted_iota(jnp.int32, (GM_CHUNK, GM_CHUNK), 1)
    for g in range(GM_GROUPS):
        cols = slice(g * gw, (g + 1) * gw)
        wg = jnp.where(tc <= tr, ws_ref[g], 0.0).astype(BF16)
        for c in range(TQ // GM_CHUNK):
            rows = slice(c * GM_CHUNK, (c + 1) * GM_CHUNK)
            s = jnp.dot(wg, vln_ref[0, rows, cols], preferred_element_type=F32) + bs_ref[:, cols]
            br_scr[0, rows, cols] = (u_ref[0, rows, cols].astype(F32) * s).astype(BF16)

    pool_scr[0:POOL_HALO, :] = jnp.where(i > 0, halo_ref[0].astype(F32), 0.0)
    pool_scr[POOL_HALO:POOL_HALO + TQ, :] = p_ref[0].astype(F32)
    pos = i * TQ + lax.broadcasted_iota(jnp.int32, (TQ, 1), 0)
    for g, w in enumerate(POOL_WINDOWS):
        cols = slice(g * POOL_GROUP_WIDTH, (g + 1) * POOL_GROUP_WIDTH)
        cur = pool_scr[POOL_HALO:POOL_HALO + TQ, cols]
        win = cur
        for back in range(1, w):
            win = win + pool_scr[POOL_HALO - back:POOL_HALO - back + TQ, cols]
        count = jnp.minimum(pos + 1, w).astype(F32)
        d = win / count - cur
        y = jnp.dot(d.astype(BF16), pw_ref[g], preferred_element_type=F32) * ps_ref[:, cols]
        br_scr[2, :, cols] = y.astype(BF16)

    merged = None
    for n in range(N_BRANCH):
        t = jnp.dot(br_scr[n], wbr_ref[n], preferred_element_type=F32)
        t = t * gate_ref[0, :, n * D_MODEL:(n + 1) * D_MODEL].astype(F32)
        merged = t if merged is None else merged + t
    y = jnp.dot(merged.astype(BF16), wout_ref[...], preferred_element_type=F32)
    o_ref[0] = x_ref[0] + gt_ref[0] * y


def _mix_call(x, u, vln, q0, q1, k, v, p, gates, gt, ws, bs_full, pw, ps, wbr, wout, tri):
    batch, seq, _ = x.shape
    bw = BRANCH_WIDTH
    rows = lambda b, i: (b, i, 0)
    whole = lambda b, i: (b, 0, 0)
    halo = lambda b, i: (b, jnp.maximum(i * (TQ // POOL_HALO) - 1, 0), 0)
    piece = pl.BlockSpec((1, TQ, bw), rows)
    return pl.pallas_call(
        _mix_kernel,
        grid=(batch, seq // TQ),
        in_specs=[
            pl.BlockSpec((1, TQ, D_MODEL), rows),
            piece, piece, piece, piece,
            pl.BlockSpec((1, seq, bw), whole),
            pl.BlockSpec((1, seq, bw), whole),
            piece,
            pl.BlockSpec((1, POOL_HALO, bw), halo),
            pl.BlockSpec((1, TQ, N_BRANCH * D_MODEL), rows),
            pl.BlockSpec((1, 1, D_MODEL), whole),
            _resident((GM_GROUPS, GM_CHUNK, GM_CHUNK)),
            _resident((GM_CHUNK, bw)),
            _resident((len(POOL_WINDOWS), POOL_GROUP_WIDTH, POOL_GROUP_WIDTH)),
            _resident((1, bw)),
            _resident((N_BRANCH, bw, D_MODEL)),
            _resident((D_MODEL, D_MODEL)),
            _resident((TK, TK)),
        ],
        out_specs=pl.BlockSpec((1, TQ, D_MODEL), rows),
        out_shape=jax.ShapeDtypeStruct(x.shape, F32),
        scratch_shapes=[
            pltpu.VMEM((N_PAIRS, TQ, HEAD_PAIR), F32),
            pltpu.VMEM((SB_HEADS, TQ, LANES), F32),
            pltpu.VMEM((POOL_HALO + TQ, bw), F32),
            pltpu.VMEM((N_BRANCH, TQ, bw), BF16),
        ],
        compiler_params=pltpu.CompilerParams(
            dimension_semantics=("arbitrary", "arbitrary"), vmem_limit_bytes=VMEM_LIMIT),
        name="token_mix",
    )(x, u, vln, q0, q1, k, v, p, p, gates, gt, ws, bs_full, pw, ps, wbr, wout, tri)


def _ffn_kernel(x_ref, g_ref, sc_ref, sh_ref, gt_ref, win_ref, wout_ref, fg_ref, o_ref, hid_scr, *, final):
    x = x_ref[...]
    h = _rms_modulate(x, g_ref[...], sc_ref[0], sh_ref[0]).astype(BF16)
    for c in range(D_FF // FF_CHUNK):
        c0 = c * FF_CHUNK
        f_gate = jnp.dot(h, win_ref[:, c0:c0 + FF_CHUNK], preferred_element_type=F32)
        f_up = jnp.dot(h, win_ref[:, D_FF + c0:D_FF + c0 + FF_CHUNK], preferred_element_type=F32)
        hid_scr[:, c0:c0 + FF_CHUNK] = (f_gate * _sigmoid(f_gate) * f_up).astype(BF16)
    y = jnp.dot(hid_scr[...], wout_ref[...], preferred_element_type=F32)
    xn = x + gt_ref[0] * y
    if final:
        ms = jnp.mean(xn * xn, axis=-1, keepdims=True)
        xn = (xn * lax.rsqrt(ms + EPS)) * fg_ref[...]
    o_ref[...] = xn


def _ffn_call(x2d, rms_g, sc, sh, gt, w_in, w_out, final_g, seq, final):
    m = x2d.shape[0]
    tm = TM_PROJ
    steps_per_seq = seq // tm
    row = lambda i: (i, 0)
    mod = lambda i: (i // steps_per_seq, 0, 0)
    return pl.pallas_call(
        functools.partial(_ffn_kernel, final=final),
        grid=(m // tm,),
        in_specs=[
            pl.BlockSpec((tm, D_MODEL), row),
            _resident((1, D_MODEL)),
            pl.BlockSpec((1, 1, D_MODEL), mod),
            pl.BlockSpec((1, 1, D_MODEL), mod),
            pl.BlockSpec((1, 1, D_MODEL), mod),
            _resident((D_MODEL, 2 * D_FF)),
            _resident((D_FF, D_MODEL)),
            _resident((1, D_MODEL)),
        ],
        out_specs=pl.BlockSpec((tm, D_MODEL), row),
        out_shape=jax.ShapeDtypeStruct(x2d.shape, F32),
        scratch_shapes=[pltpu.VMEM((tm, D_FF), BF16)],
        compiler_params=pltpu.CompilerParams(
            dimension_semantics=("arbitrary",), vmem_limit_bytes=VMEM_LIMIT),
        name="ffn_final" if final else "ffn",
    )(x2d, rms_g, sc, sh, gt, w_in, w_out, final_g)


def _suffix_matrix():
    j = lax.broadcasted_iota(jnp.int32, (TK, TK), 0)
    s = lax.broadcasted_iota(jnp.int32, (TK, TK), 1)
    return (j > s).astype(BF16)


def kernel(x, c, rms_g1, rms_g2, w_ada, b_ada, w_in, gm_ln_g, gm_ln_b, gm_w_spatial, gm_b_spatial,
           pool_w, pool_scale, w_branch, w_out, w_ffn_in, w_ffn_out, final_g):
    batch, seq, d = x.shape
    assert d == D_MODEL and seq % TM_PROJ == 0 and seq % TQ == 0 and TQ == TK and TQ % GM_CHUNK == 0
    m = batch * seq

    mod = _ada_call(c, w_ada, b_ada)
    mod = mod.reshape(DEPTH, batch, N_MOD, 1, D_MODEL)
    tri = _suffix_matrix()
    final_row = final_g.reshape(1, D_MODEL)

    for l in range(DEPTH):
        sh1, sc1, gt1, sh2, sc2, gt2 = [mod[l, :, n] for n in range(N_MOD)]
        u, vln, q0, q1, k, v, p, gates = _inproj_call(
            x.reshape(m, D_MODEL), rms_g1[l].reshape(1, D_MODEL), sc1, sh1, w_in[l].astype(BF16),
            gm_ln_g[l].reshape(1, BRANCH_WIDTH), gm_ln_b[l].reshape(1, BRANCH_WIDTH), seq)
        to_seq = lambda a: a.reshape(batch, seq, a.shape[-1])
        bs_full = jnp.repeat(gm_b_spatial[l].T, GM_GROUP_WIDTH, axis=1)
        x = _mix_call(
            x, to_seq(u), to_seq(vln), to_seq(q0), to_seq(q1), to_seq(k), to_seq(v), to_seq(p),
            to_seq(gates), gt1, gm_w_spatial[l], bs_full, pool_w[l].astype(BF16),
            pool_scale[l].reshape(1, BRANCH_WIDTH), w_branch[l].astype(BF16), w_out[l].astype(BF16), tri)
        x = _ffn_call(
            x.reshape(m, D_MODEL), rms_g2[l].reshape(1, D_MODEL), sc2, sh2, gt2,
            w_ffn_in[l].astype(BF16), w_ffn_out[l].astype(BF16), final_row, seq,
            final=(l == DEPTH - 1)).reshape(batch, seq, D_MODEL)
    return x
```

```python
import functools
import math

import jax
import jax.numpy as jnp
from jax import lax
from jax.experimental import pallas as pl
from jax.experimental.pallas import tpu as pltpu

F32 = jnp.float32
BF16 = jnp.bfloat16

D_MODEL = 1024
DEPTH = 4
BRANCH_WIDTH = D_MODEL // 2
N_BRANCH = 3
GM_CHUNK = 128
GM_GROUPS = 4
GM_GROUP_WIDTH = BRANCH_WIDTH // GM_GROUPS
SB_HEAD_DIM = 64
SB_HEADS = BRANCH_WIDTH // SB_HEAD_DIM
POOL_WINDOWS = (2, 4, 8, 16)
POOL_GROUP_WIDTH = BRANCH_WIDTH // len(POOL_WINDOWS)
D_FF = -(-8 * D_MODEL // (3 * 256)) * 256
N_MOD = 6
EPS = 1e-6
IN_COLS = 6 * BRANCH_WIDTH + N_BRANCH * D_MODEL

LANES = 128
MXU_COLS = 256
HEAD_PAIR = 2 * SB_HEAD_DIM
N_PAIRS = SB_HEADS // 2
POOL_HALO = 16
LOG2E = 1.4426950408889634
LOG2_F32_UNDERFLOW = -152.0
VMEM_LIMIT = 56 * 1024 * 1024

TM_PROJ = 1024
TQ = 256
TK = 256
FF_CHUNK = 256


def _resident(shape):
    nd = len(shape)
    return pl.BlockSpec(shape, lambda *_: (0,) * nd, pipeline_mode=pl.Buffered(1))


def _sigmoid(x):
    return 1.0 / (1.0 + jnp.exp(-x))


def _gelu_tanh(x):
    return 0.5 * x * (1.0 + jnp.tanh(0.7978845608028654 * (x + 0.044715 * (x * x * x))))


def _rms_modulate(x, g, sc, sh):
    ms = jnp.mean(x * x, axis=-1, keepdims=True)
    return (x * lax.rsqrt(ms + EPS)) * g * (1.0 + sc) + sh


def _ada_kernel(c_ref, w_ref, b_ref, o_ref):
    c = c_ref[...]
    c_act = c * _sigmoid(c)
    o_ref[0] = jnp.dot(c_act.astype(BF16), w_ref[0].astype(BF16), preferred_element_type=F32) + b_ref[0]


def _ada_call(c, w_ada, b_ada):
    batch = c.shape[0]
    n_out = N_MOD * D_MODEL
    tn = n_out // 4
    return pl.pallas_call(
        _ada_kernel,
        grid=(DEPTH, n_out // tn),
        in_specs=[
            pl.BlockSpec((batch, D_MODEL), lambda l, n: (0, 0)),
            pl.BlockSpec((1, D_MODEL, tn), lambda l, n: (l, 0, n)),
            pl.BlockSpec((1, 1, tn), lambda l, n: (l, 0, n)),
        ],
        out_specs=pl.BlockSpec((1, batch, tn), lambda l, n: (l, 0, n)),
        out_shape=jax.ShapeDtypeStruct((DEPTH, batch, n_out), F32),
        compiler_params=pltpu.CompilerParams(
            dimension_semantics=("arbitrary", "arbitrary"), vmem_limit_bytes=VMEM_LIMIT),
        name="ada_mod",
    )(c, w_ada, b_ada.reshape(DEPTH, 1, n_out))


def _inproj_kernel(x_ref, g_ref, sc_ref, sh_ref, w_ref, lng_ref, lnb_ref,
                   u_ref, vln_ref, q0_ref, q1_ref, k_ref, v_ref, p_ref, gate_ref):
    h = _rms_modulate(x_ref[...], g_ref[...], sc_ref[0], sh_ref[0]).astype(BF16)
    bw = BRANCH_WIDTH

    nc = MXU_COLS
    halves = [slice(c * nc, (c + 1) * nc) for c in range(bw // nc)]

    def proj(col0):
        return jnp.dot(h, w_ref[:, col0:col0 + nc], preferred_element_type=F32)

    for c in range(N_BRANCH * D_MODEL // nc):
        gate_ref[:, c * nc:(c + 1) * nc] = _sigmoid(proj(6 * bw + c * nc)).astype(BF16)

    for c, cols in enumerate(halves):
        u_ref[:, cols] = _gelu_tanh(proj(c * nc)).astype(BF16)

    gv = [_gelu_tanh(proj(bw + c * nc)) for c in range(len(halves))]
    mu = sum(jnp.sum(g, axis=-1, keepdims=True) for g in gv) * (1.0 / bw)
    gc = [g - mu for g in gv]
    var = sum(jnp.sum(g * g, axis=-1, keepdims=True) for g in gc) * (1.0 / bw)
    inv = lax.rsqrt(var + EPS)
    for g, cols in zip(gc, halves):
        vln_ref[:, cols] = ((g * inv) * lng_ref[:, cols] + lnb_ref[:, cols]).astype(BF16)

    lane = lax.broadcasted_iota(jnp.int32, (x_ref.shape[0], nc), 1)
    first_of_pair = (lane & SB_HEAD_DIM) == 0
    for c, cols in enumerate(halves):
        q = proj(2 * bw + c * nc) * (SB_HEAD_DIM ** -0.5 * LOG2E)
        q0_ref[:, cols] = jnp.where(first_of_pair, q, 0.0).astype(BF16)
        q1_ref[:, cols] = jnp.where(first_of_pair, 0.0, q).astype(BF16)

    for n, ref in enumerate((k_ref, v_ref, p_ref)):
        for c, cols in enumerate(halves):
            ref[:, cols] = proj((3 + n) * bw + c * nc).astype(BF16)


def _inproj_call(x2d, rms_g, sc, sh, w_in, ln_g, ln_b, seq):
    m = x2d.shape[0]
    tm = TM_PROJ
    steps_per_seq = seq // tm
    row = lambda i: (i, 0)
    mod = lambda i: (i // steps_per_seq, 0, 0)
    piece = jax.ShapeDtypeStruct((m, BRANCH_WIDTH), BF16)
    return pl.pallas_call(
        _inproj_kernel,
        grid=(m // tm,),
        in_specs=[
            pl.BlockSpec((tm, D_MODEL), row),
            _resident((1, D_MODEL)),
            pl.BlockSpec((1, 1, D_MODEL), mod),
            pl.BlockSpec((1, 1, D_MODEL), mod),
            _resident((D_MODEL, IN_COLS)),
            _resident((1, BRANCH_WIDTH)),
            _resident((1, BRANCH_WIDTH)),
        ],
        out_specs=[pl.BlockSpec((tm, BRANCH_WIDTH), row)] * 7
        + [pl.BlockSpec((tm, N_BRANCH * D_MODEL), row)],
        out_shape=[piece] * 7 + [jax.ShapeDtypeStruct((m, N_BRANCH * D_MODEL), BF16)],
        compiler_params=pltpu.CompilerParams(
            dimension_semantics=("arbitrary",), vmem_limit_bytes=VMEM_LIMIT),
        name="in_proj",
    )(x2d, rms_g, sc, sh, w_in, ln_g, ln_b)


def _mix_kernel(x_ref, u_ref, vln_ref, q0_ref, q1_ref, k_ref, v_ref, p_ref, halo_ref, gate_ref, gt_ref,
                ws_ref, bs_ref, pw_ref, ps_ref, wbr_ref, wout_ref, tri_ref,
                o_ref, acc_ref, run_ref, pool_scr, br_scr, merged_scr):
    i = pl.program_id(1)
    gw = GM_GROUP_WIDTH
    nc = MXU_COLS

    def gmlp_group(g):
        tr = lax.broadcasted_iota(jnp.int32, (GM_CHUNK, GM_CHUNK), 0)
        tc = lax.broadcasted_iota(jnp.int32, (GM_CHUNK, GM_CHUNK), 1)
        cols = slice(g * gw, (g + 1) * gw)
        wg = jnp.where(tc <= tr, ws_ref[g], 0.0).astype(BF16)
        for c in range(TQ // GM_CHUNK):
            rows = slice(c * GM_CHUNK, (c + 1) * GM_CHUNK)
            s = jnp.dot(wg, vln_ref[0, rows, cols], preferred_element_type=F32) + bs_ref[:, cols]
            br_scr[0, rows, cols] = (u_ref[0, rows, cols].astype(F32) * s).astype(BF16)

    def pool_group(g):
        w = POOL_WINDOWS[g]
        cols = slice(g * POOL_GROUP_WIDTH, (g + 1) * POOL_GROUP_WIDTH)
        pool_scr[0:POOL_HALO, cols] = jnp.where(i > 0, halo_ref[0, :, cols].astype(F32), 0.0)
        pool_scr[POOL_HALO:POOL_HALO + TQ, cols] = p_ref[0, :, cols].astype(F32)
        cur = pool_scr[POOL_HALO:POOL_HALO + TQ, cols]
        win = cur
        for back in range(1, w):
            win = win + pool_scr[POOL_HALO - back:POOL_HALO - back + TQ, cols]
        pos = i * TQ + lax.broadcasted_iota(jnp.int32, (TQ, 1), 0)
        count = jnp.minimum(pos + 1, w).astype(F32)
        d = win / count - cur
        y = jnp.dot(d.astype(BF16), pw_ref[g], preferred_element_type=F32) * ps_ref[:, cols]
        br_scr[2, :, cols] = y.astype(BF16)

    def gated_branch(n, cols):
        t = jnp.dot(br_scr[n], wbr_ref[n, :, cols], preferred_element_type=F32)
        return t * gate_ref[0, :, n * D_MODEL + cols.start:n * D_MODEL + cols.stop].astype(F32)

    def merge_local(c):
        cols = slice(c * nc, (c + 1) * nc)
        merged_scr[:, cols] = gated_branch(0, cols) + gated_branch(2, cols)

    def local_work():
        work = []
        for g in range(GM_GROUPS):
            work += [functools.partial(gmlp_group, g), functools.partial(pool_group, g)]
        return work + [functools.partial(merge_local, c) for c in range(D_MODEL // nc)]

    row_id = lax.broadcasted_iota(jnp.int32, (TQ, TK), 0)
    col_id = lax.broadcasted_iota(jnp.int32, (TQ, TK), 1)
    strictly_causal = col_id < row_id
    first_of_pair = lax.broadcasted_iota(jnp.int32, (TQ, HEAD_PAIR), 1) < SB_HEAD_DIM

    q_refs = (q0_ref, q1_ref)
    pair_cols = lambda head: slice((head // 2) * HEAD_PAIR, (head // 2 + 1) * HEAD_PAIR)

    def sweep(blocks, side_work=()):
        items = [(j, diagonal, head) for j, diagonal in blocks for head in range(SB_HEADS)]
        side_work = list(side_work)
        n_steps = len(items) + 2
        side_per_step = -(-len(side_work) // n_steps)
        log_beta, split, sums, weights, av, running = {}, {}, {}, {}, {}, {}
        key_rows = lambda j: pl.ds(pl.multiple_of(j * TK, TK), TK)

        def scores(n):
            j, diagonal, head = items[n]
            z = lax.dot_general(q_refs[head % 2][0, :, pair_cols(head)], k_ref[0, key_rows(j), pair_cols(head)],
                                (((1,), (1,)), ((), ())), preferred_element_type=F32)
            neg_abs = lax.bitcast_convert_type(
                lax.bitcast_convert_type(z, jnp.int32) | jnp.int32(-2 ** 31), F32)
            lse = jnp.log(1.0 + jnp.exp2(neg_abs)) * LOG2E
            log_beta[n] = jnp.minimum(z, 0.0) - lse
            log_rest = log_beta[n] - z
            if diagonal:
                log_rest = jnp.where(strictly_causal, log_rest, 0.0)
            split[n] = log_rest.astype(BF16)
            sums[n] = jnp.broadcast_to(jnp.sum(log_rest, axis=1, keepdims=True), (TQ, LANES))

        def suffix_and_weights(n):
            j, diagonal, head = items[n]
            suffix = jnp.dot(split.pop(n), tri_ref[...], preferred_element_type=F32)
            total = log_beta.pop(n) + suffix
            if diagonal:
                running[head] = sums.pop(n)
            else:
                later = run_ref[head]
                total = total + jnp.concatenate([later] * (TK // LANES), axis=1)
                running[head] = later + sums.pop(n)
            run_ref[head] = running[head]
            a = jnp.exp2(total)
            if diagonal:
                a = jnp.where(strictly_causal, a, 0.0)
            weights[n] = a.astype(BF16)

        def weighted_values(n):
            j, diagonal, head = items[n]
            av[n] = jnp.dot(weights.pop(n), v_ref[0, key_rows(j), pair_cols(head)], preferred_element_type=F32)
            if head % 2:
                both = jnp.where(first_of_pair, av.pop(n - 1), av.pop(n))
                pair = head // 2
                acc_ref[pair] = both if diagonal else acc_ref[pair] + both

        for step in range(n_steps):
            if step < len(items):
                scores(step)
            if 1 <= step <= len(items):
                suffix_and_weights(step - 1)
            if step >= 2:
                weighted_values(step - 2)
            for _ in range(side_per_step):
                if side_work:
                    side_work.pop(0)()
        least_decayed = functools.reduce(jnp.maximum, [running[h] for h in range(SB_HEADS)])
        return (jnp.max(least_decayed) > LOG2_F32_UNDERFLOW).astype(jnp.int32)

    live = lax.cond(i == 0,
                    lambda: sweep([(i, True)], local_work()),
                    lambda: sweep([(i, True), (i - 1, False)], local_work()))

    def earlier_block(carry):
        t, _ = carry
        return t + 1, sweep([(i - 1 - t, False)])

    lax.while_loop(lambda carry: (carry[0] < i) & (carry[1] > 0), earlier_block, (jnp.int32(1), live))
    for p in range(N_PAIRS):
        br_scr[1, :, p * HEAD_PAIR:(p + 1) * HEAD_PAIR] = acc_ref[p].astype(BF16)

    for c in range(D_MODEL // nc):
        cols = slice(c * nc, (c + 1) * nc)
        merged_scr[:, cols] = merged_scr[:, cols] + gated_branch(1, cols)
    y = jnp.dot(merged_scr[...].astype(BF16), wout_ref[...], preferred_element_type=F32)
    o_ref[0] = x_ref[0] + gt_ref[0] * y


def _mix_call(x, u, vln, q0, q1, k, v, p, gates, gt, ws, bs_full, pw, ps, wbr, wout, tri):
    batch, seq, _ = x.shape
    bw = BRANCH_WIDTH
    rows = lambda b, i: (b, i, 0)
    whole = lambda b, i: (b, 0, 0)
    halo = lambda b, i: (b, jnp.maximum(i * (TQ // POOL_HALO) - 1, 0), 0)
    piece = pl.BlockSpec((1, TQ, bw), rows)
    return pl.pallas_call(
        _mix_kernel,
        grid=(batch, seq // TQ),
        in_specs=[
            pl.BlockSpec((1, TQ, D_MODEL), rows),
            piece, piece, piece, piece,
            pl.BlockSpec((1, seq, bw), whole),
            pl.BlockSpec((1, seq, bw), whole),
            piece,
            pl.BlockSpec((1, POOL_HALO, bw), halo),
            pl.BlockSpec((1, TQ, N_BRANCH * D_MODEL), rows),
            pl.BlockSpec((1, 1, D_MODEL), whole),
            _resident((GM_GROUPS, GM_CHUNK, GM_CHUNK)),
            _resident((GM_CHUNK, bw)),
            _resident((len(POOL_WINDOWS), POOL_GROUP_WIDTH, POOL_GROUP_WIDTH)),
            _resident((1, bw)),
            _resident((N_BRANCH, bw, D_MODEL)),
            _resident((D_MODEL, D_MODEL)),
            _resident((TK, TK)),
        ],
        out_specs=pl.BlockSpec((1, TQ, D_MODEL), rows),
        out_shape=jax.ShapeDtypeStruct(x.shape, F32),
        scratch_shapes=[
            pltpu.VMEM((N_PAIRS, TQ, HEAD_PAIR), F32),
            pltpu.VMEM((SB_HEADS, TQ, LANES), F32),
            pltpu.VMEM((POOL_HALO + TQ, bw), F32),
            pltpu.VMEM((N_BRANCH, TQ, bw), BF16),
            pltpu.VMEM((TQ, D_MODEL), F32),
        ],
        compiler_params=pltpu.CompilerParams(
            dimension_semantics=("arbitrary", "arbitrary"), vmem_limit_bytes=VMEM_LIMIT),
        name="token_mix",
    )(x, u, vln, q0, q1, k, v, p, p, gates, gt, ws, bs_full, pw, ps, wbr, wout, tri)


def _ffn_kernel(x_ref, g_ref, sc_ref, sh_ref, gt_ref, win_ref, wout_ref, fg_ref, o_ref, hid_scr, *, final):
    x = x_ref[...]
    h = _rms_modulate(x, g_ref[...], sc_ref[0], sh_ref[0]).astype(BF16)
    for c in range(D_FF // FF_CHUNK):
        c0 = c * FF_CHUNK
        f_gate = jnp.dot(h, win_ref[:, c0:c0 + FF_CHUNK], preferred_element_type=F32)
        f_up = jnp.dot(h, win_ref[:, D_FF + c0:D_FF + c0 + FF_CHUNK], preferred_element_type=F32)
        hid_scr[:, c0:c0 + FF_CHUNK] = (f_gate * _sigmoid(f_gate) * f_up).astype(BF16)
    y = jnp.dot(hid_scr[...], wout_ref[...], preferred_element_type=F32)
    xn = x + gt_ref[0] * y
    if final:
        ms = jnp.mean(xn * xn, axis=-1, keepdims=True)
        xn = (xn * lax.rsqrt(ms + EPS)) * fg_ref[...]
    o_ref[...] = xn


def _ffn_call(x2d, rms_g, sc, sh, gt, w_in, w_out, final_g, seq, final):
    m = x2d.shape[0]
    tm = TM_PROJ
    steps_per_seq = seq // tm
    row = lambda i: (i, 0)
    mod = lambda i: (i // steps_per_seq, 0, 0)
    return pl.pallas_call(
        functools.partial(_ffn_kernel, final=final),
        grid=(m // tm,),
        in_specs=[
            pl.BlockSpec((tm, D_MODEL), row),
            _resident((1, D_MODEL)),
            pl.BlockSpec((1, 1, D_MODEL), mod),
            pl.BlockSpec((1, 1, D_MODEL), mod),
            pl.BlockSpec((1, 1, D_MODEL), mod),
            _resident((D_MODEL, 2 * D_FF)),
            _resident((D_FF, D_MODEL)),
            _resident((1, D_MODEL)),
        ],
        out_specs=pl.BlockSpec((tm, D_MODEL), row),
        out_shape=jax.ShapeDtypeStruct(x2d.shape, F32),
        scratch_shapes=[pltpu.VMEM((tm, D_FF), BF16)],
        compiler_params=pltpu.CompilerParams(
            dimension_semantics=("arbitrary",), vmem_limit_bytes=VMEM_LIMIT),
        name="ffn_final" if final else "ffn",
    )(x2d, rms_g, sc, sh, gt, w_in, w_out, final_g)


def _suffix_matrix():
    j = lax.broadcasted_iota(jnp.int32, (TK, TK), 0)
    s = lax.broadcasted_iota(jnp.int32, (TK, TK), 1)
    return (j > s).astype(BF16)


def kernel(x, c, rms_g1, rms_g2, w_ada, b_ada, w_in, gm_ln_g, gm_ln_b, gm_w_spatial, gm_b_spatial,
           pool_w, pool_scale, w_branch, w_out, w_ffn_in, w_ffn_out, final_g):
    batch, seq, d = x.shape
    assert d == D_MODEL and seq % TM_PROJ == 0 and seq % TQ == 0 and TQ == TK and TQ % GM_CHUNK == 0
    m = batch * seq

    mod = _ada_call(c, w_ada, b_ada)
    mod = mod.reshape(DEPTH, batch, N_MOD, 1, D_MODEL)
    tri = _suffix_matrix()
    final_row = final_g.reshape(1, D_MODEL)

    for l in range(DEPTH):
        sh1, sc1, gt1, sh2, sc2, gt2 = [mod[l, :, n] for n in range(N_MOD)]
        u, vln, q0, q1, k, v, p, gates = _inproj_call(
            x.reshape(m, D_MODEL), rms_g1[l].reshape(1, D_MODEL), sc1, sh1, w_in[l].astype(BF16),
            gm_ln_g[l].reshape(1, BRANCH_WIDTH), gm_ln_b[l].reshape(1, BRANCH_WIDTH), seq)
        to_seq = lambda a: a.reshape(batch, seq, a.shape[-1])
        bs_full = jnp.repeat(gm_b_spatial[l].T, GM_GROUP_WIDTH, axis=1)
        x = _mix_call(
            x, to_seq(u), to_seq(vln), to_seq(q0), to_seq(q1), to_seq(k), to_seq(v), to_seq(p),
            to_seq(gates), gt1, gm_w_spatial[l], bs_full, pool_w[l].astype(BF16),
            pool_scale[l].reshape(1, BRANCH_WIDTH), w_branch[l].astype(BF16), w_out[l].astype(BF16), tri)
        x = _ffn_call(
            x.reshape(m, D_MODEL), rms_g2[l].reshape(1, D_MODEL), sc2, sh2, gt2,
            w_ffn_in[l].astype(BF16), w_ffn_out[l].astype(BF16), final_row, seq,
            final=(l == DEPTH - 1)).reshape(batch, seq, D_MODEL)
    return x
```

```python
import functools
import math

import jax
import jax.numpy as jnp
from jax import lax
from jax.experimental import pallas as pl
from jax.experimental.pallas import tpu as pltpu

F32 = jnp.float32
BF16 = jnp.bfloat16

D_MODEL = 1024
DEPTH = 4
BRANCH_WIDTH = D_MODEL // 2
N_BRANCH = 3
GM_CHUNK = 128
GM_GROUPS = 4
GM_GROUP_WIDTH = BRANCH_WIDTH // GM_GROUPS
SB_HEAD_DIM = 64
SB_HEADS = BRANCH_WIDTH // SB_HEAD_DIM
POOL_WINDOWS = (2, 4, 8, 16)
POOL_GROUP_WIDTH = BRANCH_WIDTH // len(POOL_WINDOWS)
D_FF = -(-8 * D_MODEL // (3 * 256)) * 256
N_MOD = 6
EPS = 1e-6
IN_COLS = 6 * BRANCH_WIDTH + N_BRANCH * D_MODEL

LANES = 128
MXU_COLS = 256
HEAD_PAIR = 2 * SB_HEAD_DIM
N_PAIRS = SB_HEADS // 2
POOL_HALO = 16
LOG2E = 1.4426950408889634
LOG2_F32_UNDERFLOW = -152.0
VMEM_LIMIT = 56 * 1024 * 1024

TM_PROJ = 1024
TQ = 256
TK = 256
PREV_BLOCK_LEAD_ROWS = 160
PIPE_SKEW = 2
FF_CHUNK = 256


def _resident(shape):
    nd = len(shape)
    return pl.BlockSpec(shape, lambda *_: (0,) * nd, pipeline_mode=pl.Buffered(1))


def _sigmoid(x):
    return 1.0 / (1.0 + jnp.exp(-x))


def _gelu_tanh(x):
    return 0.5 * x * (1.0 + jnp.tanh(0.7978845608028654 * (x + 0.044715 * (x * x * x))))


def _rms_modulate(x, g, sc, sh):
    ms = jnp.mean(x * x, axis=-1, keepdims=True)
    return (x * lax.rsqrt(ms + EPS)) * g * (1.0 + sc) + sh


def _ada_kernel(c_ref, w_ref, b_ref, o_ref):
    c = c_ref[...]
    c_act = c * _sigmoid(c)
    o_ref[0] = jnp.dot(c_act.astype(BF16), w_ref[0].astype(BF16), preferred_element_type=F32) + b_ref[0]


def _ada_call(c, w_ada, b_ada):
    batch = c.shape[0]
    n_out = N_MOD * D_MODEL
    tn = n_out // 4
    return pl.pallas_call(
        _ada_kernel,
        grid=(DEPTH, n_out // tn),
        in_specs=[
            pl.BlockSpec((batch, D_MODEL), lambda l, n: (0, 0)),
            pl.BlockSpec((1, D_MODEL, tn), lambda l, n: (l, 0, n)),
            pl.BlockSpec((1, 1, tn), lambda l, n: (l, 0, n)),
        ],
        out_specs=pl.BlockSpec((1, batch, tn), lambda l, n: (l, 0, n)),
        out_shape=jax.ShapeDtypeStruct((DEPTH, batch, n_out), F32),
        compiler_params=pltpu.CompilerParams(
            dimension_semantics=("arbitrary", "arbitrary"), vmem_limit_bytes=VMEM_LIMIT),
        name="ada_mod",
    )(c, w_ada, b_ada.reshape(DEPTH, 1, n_out))


def _inproj_kernel(x_ref, g_ref, sc_ref, sh_ref, w_ref, lng_ref, lnb_ref,
                   u_ref, vln_ref, q0_ref, q1_ref, k_ref, v_ref, p_ref, gate_ref):
    h = _rms_modulate(x_ref[...], g_ref[...], sc_ref[0], sh_ref[0]).astype(BF16)
    bw = BRANCH_WIDTH

    nc = MXU_COLS
    halves = [slice(c * nc, (c + 1) * nc) for c in range(bw // nc)]

    def proj(col0):
        return jnp.dot(h, w_ref[:, col0:col0 + nc], preferred_element_type=F32)

    for c in range(N_BRANCH * D_MODEL // nc):
        gate_ref[:, c * nc:(c + 1) * nc] = _sigmoid(proj(6 * bw + c * nc)).astype(BF16)

    for c, cols in enumerate(halves):
        u_ref[:, cols] = _gelu_tanh(proj(c * nc)).astype(BF16)

    gv = [_gelu_tanh(proj(bw + c * nc)) for c in range(len(halves))]
    mu = sum(jnp.sum(g, axis=-1, keepdims=True) for g in gv) * (1.0 / bw)
    gc = [g - mu for g in gv]
    var = sum(jnp.sum(g * g, axis=-1, keepdims=True) for g in gc) * (1.0 / bw)
    inv = lax.rsqrt(var + EPS)
    for g, cols in zip(gc, halves):
        vln_ref[:, cols] = ((g * inv) * lng_ref[:, cols] + lnb_ref[:, cols]).astype(BF16)

    lane = lax.broadcasted_iota(jnp.int32, (x_ref.shape[0], nc), 1)
    first_of_pair = (lane & SB_HEAD_DIM) == 0
    for c, cols in enumerate(halves):
        q = proj(2 * bw + c * nc) * (SB_HEAD_DIM ** -0.5 * LOG2E)
        q0_ref[:, cols] = jnp.where(first_of_pair, q, 0.0).astype(BF16)
        q1_ref[:, cols] = jnp.where(first_of_pair, 0.0, q).astype(BF16)

    for n, ref in enumerate((k_ref, v_ref, p_ref)):
        for c, cols in enumerate(halves):
            ref[:, cols] = proj((3 + n) * bw + c * nc).astype(BF16)


def _inproj_call(x2d, rms_g, sc, sh, w_in, ln_g, ln_b, seq):
    m = x2d.shape[0]
    tm = TM_PROJ
    steps_per_seq = seq // tm
    row = lambda i: (i, 0)
    mod = lambda i: (i // steps_per_seq, 0, 0)
    piece = jax.ShapeDtypeStruct((m, BRANCH_WIDTH), BF16)
    return pl.pallas_call(
        _inproj_kernel,
        grid=(m // tm,),
        in_specs=[
            pl.BlockSpec((tm, D_MODEL), row),
            _resident((1, D_MODEL)),
            pl.BlockSpec((1, 1, D_MODEL), mod),
            pl.BlockSpec((1, 1, D_MODEL), mod),
            _resident((D_MODEL, IN_COLS)),
            _resident((1, BRANCH_WIDTH)),
            _resident((1, BRANCH_WIDTH)),
        ],
        out_specs=[pl.BlockSpec((tm, BRANCH_WIDTH), row)] * 7
        + [pl.BlockSpec((tm, N_BRANCH * D_MODEL), row)],
        out_shape=[piece] * 7 + [jax.ShapeDtypeStruct((m, N_BRANCH * D_MODEL), BF16)],
        compiler_params=pltpu.CompilerParams(
            dimension_semantics=("arbitrary",), vmem_limit_bytes=VMEM_LIMIT),
        name="in_proj",
    )(x2d, rms_g, sc, sh, w_in, ln_g, ln_b)


def _mix_kernel(x_ref, u_ref, vln_ref, q0_ref, q1_ref, k_ref, v_ref, p_ref, halo_ref, gate_ref, gt_ref,
                ws_ref, bs_ref, pw_ref, ps_ref, wbr_ref, wout_ref, tri_ref, band_ref,
                o_ref, acc_ref, run_ref, br_scr, merged_scr):
    i = pl.program_id(1)
    gw = GM_GROUP_WIDTH
    nc = MXU_COLS

    def gmlp_group(g):
        tr = lax.broadcasted_iota(jnp.int32, (GM_CHUNK, GM_CHUNK), 0)
        tc = lax.broadcasted_iota(jnp.int32, (GM_CHUNK, GM_CHUNK), 1)
        cols = slice(g * gw, (g + 1) * gw)
        wg = jnp.where(tc <= tr, ws_ref[g], 0.0).astype(BF16)
        for c in range(TQ // GM_CHUNK):
            rows = slice(c * GM_CHUNK, (c + 1) * GM_CHUNK)
            s = jnp.dot(wg, vln_ref[0, rows, cols], preferred_element_type=F32) + bs_ref[:, cols]
            br_scr[0, rows, cols] = (u_ref[0, rows, cols].astype(F32) * s).astype(BF16)

    def pool_diff(g):
        w = POOL_WINDOWS[g]
        cols = slice(g * POOL_GROUP_WIDTH, (g + 1) * POOL_GROUP_WIDTH)
        history = halo_ref[0, :, cols]
        history = jnp.where(i > 0, history, jnp.zeros_like(history))
        win = jnp.dot(band_ref[g], jnp.concatenate([history, p_ref[0, :, cols]], axis=0),
                      preferred_element_type=F32)
        cur = p_ref[0, :, cols].astype(F32)
        pos = i * TQ + lax.broadcasted_iota(jnp.int32, (TQ, 1), 0)
        count = jnp.minimum(pos + 1, w).astype(F32)
        br_scr[2, :, cols] = (win / count - cur).astype(BF16)

    def pool_map(g):
        cols = slice(g * POOL_GROUP_WIDTH, (g + 1) * POOL_GROUP_WIDTH)
        y = jnp.dot(br_scr[2, :, cols], pw_ref[g], preferred_element_type=F32) * ps_ref[:, cols]
        br_scr[2, :, cols] = y.astype(BF16)

    def gated_branch(n, cols):
        t = jnp.dot(br_scr[n], wbr_ref[n, :, cols], preferred_element_type=F32)
        return t * gate_ref[0, :, n * D_MODEL + cols.start:n * D_MODEL + cols.stop].astype(F32)

    def merge_local(c):
        cols = slice(c * nc, (c + 1) * nc)
        merged_scr[:, cols] = gated_branch(0, cols) + gated_branch(2, cols)

    def local_work():
        work = []
        for g in range(GM_GROUPS):
            work += [functools.partial(gmlp_group, g), functools.partial(pool_diff, g)]
        work += [functools.partial(pool_map, g) for g in range(len(POOL_WINDOWS))]
        return work + [functools.partial(merge_local, c) for c in range(D_MODEL // nc)]

    row_id = lax.broadcasted_iota(jnp.int32, (TQ, TK), 0)
    col_id = lax.broadcasted_iota(jnp.int32, (TQ, TK), 1)
    strictly_causal = col_id < row_id
    first_of_pair = lax.broadcasted_iota(jnp.int32, (TQ, HEAD_PAIR), 1) < SB_HEAD_DIM

    q_refs = (q0_ref, q1_ref)
    pair_cols = lambda head: slice((head // 2) * HEAD_PAIR, (head // 2 + 1) * HEAD_PAIR)

    def sweep(blocks, side_work=()):
        items = [piece + (head,) for piece in blocks for head in range(SB_HEADS)]
        side_work = list(side_work)
        n_steps = len(items) + 2 * PIPE_SKEW
        side_per_step = -(-len(side_work) // n_steps)
        log_beta, split, sums, weights, av, running = {}, {}, {}, {}, {}, {}

        def scores(n):
            j, row0, nrows, nkeys, masked, first, head = items[n]
            keys = pl.ds(pl.multiple_of(j * TK, TK), nkeys)
            z = lax.dot_general(q_refs[head % 2][0, row0:row0 + nrows, pair_cols(head)],
                                k_ref[0, keys, pair_cols(head)],
                                (((1,), (1,)), ((), ())), preferred_element_type=F32)
            neg_abs = lax.bitcast_convert_type(
                lax.bitcast_convert_type(z, jnp.int32) | jnp.int32(-2 ** 31), F32)
            lse = jnp.log(1.0 + jnp.exp2(neg_abs)) * LOG2E
            log_beta[n] = jnp.minimum(z, 0.0) - lse
            log_rest = log_beta[n] - z
            if masked:
                log_rest = jnp.where(strictly_causal[row0:row0 + nrows, :nkeys], log_rest, 0.0)
            split[n] = log_rest.astype(BF16)
            sums[n] = jnp.broadcast_to(jnp.sum(log_rest, axis=1, keepdims=True), (nrows, LANES))

        def suffix_and_weights(n):
            j, row0, nrows, nkeys, masked, first, head = items[n]
            suffix = jnp.dot(split.pop(n), tri_ref[:nkeys, :nkeys], preferred_element_type=F32)
            total = log_beta.pop(n) + suffix
            if first:
                running[n] = sums.pop(n)
            else:
                later = run_ref[head, row0:row0 + nrows]
                total = total + jnp.concatenate([later] * (nkeys // LANES), axis=1)
                running[n] = later + sums.pop(n)
            run_ref[head, row0:row0 + nrows] = running[n]
            a = jnp.exp2(total)
            if masked:
                a = jnp.where(strictly_causal[row0:row0 + nrows, :nkeys], a, 0.0)
            weights[n] = a.astype(BF16)

        def weighted_values(n):
            j, row0, nrows, nkeys, masked, first, head = items[n]
            keys = pl.ds(pl.multiple_of(j * TK, TK), nkeys)
            av[n] = jnp.dot(weights.pop(n), v_ref[0, keys, pair_cols(head)], preferred_element_type=F32)
            if head % 2:
                first_of_pair = lax.broadcasted_iota(jnp.int32, (nrows, HEAD_PAIR), 1) < SB_HEAD_DIM
                both = jnp.where(first_of_pair, av.pop(n - 1), av.pop(n))
                pair = head // 2
                rows = slice(row0, row0 + nrows)
                acc_ref[pair, rows] = both if first else acc_ref[pair, rows] + both

        for step in range(n_steps):
            if step < len(items):
                scores(step)
            if 0 <= step - PIPE_SKEW < len(items):
                suffix_and_weights(step - PIPE_SKEW)
            if step >= 2 * PIPE_SKEW:
                weighted_values(step - 2 * PIPE_SKEW)
            for _ in range(side_per_step):
                if side_work:
                    side_work.pop(0)()
        return [[running[b * SB_HEADS + head] for head in range(SB_HEADS)] for b in range(len(blocks))]

    def any_live(values):
        least_decayed = functools.reduce(jnp.maximum, values)
        return (jnp.max(least_decayed) > LOG2_F32_UNDERFLOW).astype(jnp.int32)

    half = TQ // 2
    diagonal = [(i, 0, half, half, True, True), (i, half, half, TK, True, True)]

    def first_tile():
        sweep(diagonal, local_work())
        return jnp.int32(0)

    def later_tile():
        lead = PREV_BLOCK_LEAD_ROWS
        run = sweep(diagonal + [(i - 1, 0, lead, TK, False, False)], local_work())
        rest_live = any_live([r[lead - half:] for r in run[1]])
        rest_live = lax.cond(
            rest_live > 0,
            lambda: any_live(sweep([(i - 1, lead, TQ - lead, TK, False, False)])[0]),
            lambda: jnp.int32(0))
        return any_live(run[2]) | rest_live

    live = lax.cond(i == 0, first_tile, later_tile)

    def earlier_block(carry):
        t, _ = carry
        return t + 1, any_live(sweep([(i - 1 - t, 0, TQ, TK, False, False)])[0])

    lax.while_loop(lambda carry: (carry[0] < i) & (carry[1] > 0), earlier_block, (jnp.int32(1), live))
    for p in range(N_PAIRS):
        br_scr[1, :, p * HEAD_PAIR:(p + 1) * HEAD_PAIR] = acc_ref[p].astype(BF16)

    for c in range(D_MODEL // nc):
        cols = slice(c * nc, (c + 1) * nc)
        merged_scr[:, cols] = merged_scr[:, cols] + gated_branch(1, cols)
    y = jnp.dot(merged_scr[...].astype(BF16), wout_ref[...], preferred_element_type=F32)
    o_ref[0] = x_ref[0] + gt_ref[0] * y


def _mix_call(x, u, vln, q0, q1, k, v, p, gates, gt, ws, bs_full, pw, ps, wbr, wout, tri, band):
    batch, seq, _ = x.shape
    bw = BRANCH_WIDTH
    rows = lambda b, i: (b, i, 0)
    whole = lambda b, i: (b, 0, 0)
    halo = lambda b, i: (b, jnp.maximum(i * (TQ // POOL_HALO) - 1, 0), 0)
    piece = pl.BlockSpec((1, TQ, bw), rows)
    return pl.pallas_call(
        _mix_kernel,
        grid=(batch, seq // TQ),
        in_specs=[
            pl.BlockSpec((1, TQ, D_MODEL), rows),
            piece, piece, piece, piece,
            pl.BlockSpec((1, seq, bw), whole),
            pl.BlockSpec((1, seq, bw), whole),
            piece,
            pl.BlockSpec((1, POOL_HALO, bw), halo),
            pl.BlockSpec((1, TQ, N_BRANCH * D_MODEL), rows),
            pl.BlockSpec((1, 1, D_MODEL), whole),
            _resident((GM_GROUPS, GM_CHUNK, GM_CHUNK)),
            _resident((GM_CHUNK, bw)),
            _resident((len(POOL_WINDOWS), POOL_GROUP_WIDTH, POOL_GROUP_WIDTH)),
            _resident((1, bw)),
            _resident((N_BRANCH, bw, D_MODEL)),
            _resident((D_MODEL, D_MODEL)),
            _resident((TK, TK)),
            _resident((len(POOL_WINDOWS), TQ, POOL_HALO + TQ)),
        ],
        out_specs=pl.BlockSpec((1, TQ, D_MODEL), rows),
        out_shape=jax.ShapeDtypeStruct(x.shape, F32),
        scratch_shapes=[
            pltpu.VMEM((N_PAIRS, TQ, HEAD_PAIR), F32),
            pltpu.VMEM((SB_HEADS, TQ, LANES), F32),
            pltpu.VMEM((N_BRANCH, TQ, bw), BF16),
            pltpu.VMEM((TQ, D_MODEL), F32),
        ],
        compiler_params=pltpu.CompilerParams(
            dimension_semantics=("arbitrary", "arbitrary"), vmem_limit_bytes=VMEM_LIMIT),
        name="token_mix",
    )(x, u, vln, q0, q1, k, v, p, p, gates, gt, ws, bs_full, pw, ps, wbr, wout, tri, band)


def _ffn_kernel(x_ref, g_ref, sc_ref, sh_ref, gt_ref, win_ref, wout_ref, fg_ref, o_ref, hid_scr, *, final):
    x = x_ref[...]
    h = _rms_modulate(x, g_ref[...], sc_ref[0], sh_ref[0]).astype(BF16)
    for c in range(D_FF // FF_CHUNK):
        c0 = c * FF_CHUNK
        f_gate = jnp.dot(h, win_ref[:, c0:c0 + FF_CHUNK], preferred_element_type=F32)
        f_up = jnp.dot(h, win_ref[:, D_FF + c0:D_FF + c0 + FF_CHUNK], preferred_element_type=F32)
        hid_scr[:, c0:c0 + FF_CHUNK] = (f_gate * _sigmoid(f_gate) * f_up).astype(BF16)
    y = jnp.dot(hid_scr[...], wout_ref[...], preferred_element_type=F32)
    xn = x + gt_ref[0] * y
    if final:
        ms = jnp.mean(xn * xn, axis=-1, keepdims=True)
        xn = (xn * lax.rsqrt(ms + EPS)) * fg_ref[...]
    o_ref[...] = xn


def _ffn_call(x2d, rms_g, sc, sh, gt, w_in, w_out, final_g, seq, final):
    m = x2d.shape[0]
    tm = TM_PROJ
    steps_per_seq = seq // tm
    row = lambda i: (i, 0)
    mod = lambda i: (i // steps_per_seq, 0, 0)
    return pl.pallas_call(
        functools.partial(_ffn_kernel, final=final),
        grid=(m // tm,),
        in_specs=[
            pl.BlockSpec((tm, D_MODEL), row),
            _resident((1, D_MODEL)),
            pl.BlockSpec((1, 1, D_MODEL), mod),
            pl.BlockSpec((1, 1, D_MODEL), mod),
            pl.BlockSpec((1, 1, D_MODEL), mod),
            _resident((D_MODEL, 2 * D_FF)),
            _resident((D_FF, D_MODEL)),
            _resident((1, D_MODEL)),
        ],
        out_specs=pl.BlockSpec((tm, D_MODEL), row),
        out_shape=jax.ShapeDtypeStruct(x2d.shape, F32),
        scratch_shapes=[pltpu.VMEM((tm, D_FF), BF16)],
        compiler_params=pltpu.CompilerParams(
            dimension_semantics=("arbitrary",), vmem_limit_bytes=VMEM_LIMIT),
        name="ffn_final" if final else "ffn",
    )(x2d, rms_g, sc, sh, gt, w_in, w_out, final_g)


def _suffix_matrix():
    j = lax.broadcasted_iota(jnp.int32, (TK, TK), 0)
    s = lax.broadcasted_iota(jnp.int32, (TK, TK), 1)
    return (j > s).astype(BF16)


def _window_matrices():
    t = lax.broadcasted_iota(jnp.int32, (TQ, POOL_HALO + TQ), 0)
    r = lax.broadcasted_iota(jnp.int32, (TQ, POOL_HALO + TQ), 1)
    back = t + POOL_HALO - r
    return jnp.stack([((back >= 0) & (back < w)).astype(BF16) for w in POOL_WINDOWS])


def kernel(x, c, rms_g1, rms_g2, w_ada, b_ada, w_in, gm_ln_g, gm_ln_b, gm_w_spatial, gm_b_spatial,
           pool_w, pool_scale, w_branch, w_out, w_ffn_in, w_ffn_out, final_g):
    batch, seq, d = x.shape
    assert d == D_MODEL and seq % TM_PROJ == 0 and seq % TQ == 0 and TQ == TK and TQ % GM_CHUNK == 0
    m = batch * seq

    mod = _ada_call(c, w_ada, b_ada)
    mod = mod.reshape(DEPTH, batch, N_MOD, 1, D_MODEL)
    tri = _suffix_matrix()
    band = _window_matrices()
    final_row = final_g.reshape(1, D_MODEL)

    for l in range(DEPTH):
        sh1, sc1, gt1, sh2, sc2, gt2 = [mod[l, :, n] for n in range(N_MOD)]
        u, vln, q0, q1, k, v, p, gates = _inproj_call(
            x.reshape(m, D_MODEL), rms_g1[l].reshape(1, D_MODEL), sc1, sh1, w_in[l].astype(BF16),
            gm_ln_g[l].reshape(1, BRANCH_WIDTH), gm_ln_b[l].reshape(1, BRANCH_WIDTH), seq)
        to_seq = lambda a: a.reshape(batch, seq, a.shape[-1])
        bs_full = jnp.repeat(gm_b_spatial[l].T, GM_GROUP_WIDTH, axis=1)
        x = _mix_call(
            x, to_seq(u), to_seq(vln), to_seq(q0), to_seq(q1), to_seq(k), to_seq(v), to_seq(p),
            to_seq(gates), gt1, gm_w_spatial[l], bs_full, pool_w[l].astype(BF16),
            pool_scale[l].reshape(1, BRANCH_WIDTH), w_branch[l].astype(BF16), w_out[l].astype(BF16), tri,
            band)
        x = _ffn_call(
            x.reshape(m, D_MODEL), rms_g2[l].reshape(1, D_MODEL), sc2, sh2, gt2,
            w_ffn_in[l].astype(BF16), w_ffn_out[l].astype(BF16), final_row, seq,
            final=(l == DEPTH - 1)).reshape(batch, seq, D_MODEL)
    return x
```

```python
import functools
import math

import jax
import jax.numpy as jnp
from jax import lax
from jax.experimental import pallas as pl
from jax.experimental.pallas import tpu as pltpu

F32 = jnp.float32
BF16 = jnp.bfloat16

D_MODEL = 1024
DEPTH = 4
BRANCH_WIDTH = D_MODEL // 2
N_BRANCH = 3
GM_CHUNK = 128
GM_GROUPS = 4
GM_GROUP_WIDTH = BRANCH_WIDTH // GM_GROUPS
SB_HEAD_DIM = 64
SB_HEADS = BRANCH_WIDTH // SB_HEAD_DIM
POOL_WINDOWS = (2, 4, 8, 16)
POOL_GROUP_WIDTH = BRANCH_WIDTH // len(POOL_WINDOWS)
D_FF = -(-8 * D_MODEL // (3 * 256)) * 256
N_MOD = 6
EPS = 1e-6
IN_COLS = 6 * BRANCH_WIDTH + N_BRANCH * D_MODEL

LANES = 128
MXU_COLS = 256
HEAD_PAIR = 2 * SB_HEAD_DIM
N_PAIRS = SB_HEADS // 2
POOL_HALO = 16
LOG2E = 1.4426950408889634
LOG2_F32_UNDERFLOW = -152.0
VMEM_LIMIT = 56 * 1024 * 1024

TM_PROJ = 1024
TQ = 256
TK = 256
PIPE_SKEW = 2
FF_CHUNK = 256


def _resident(shape):
    nd = len(shape)
    return pl.BlockSpec(shape, lambda *_: (0,) * nd, pipeline_mode=pl.Buffered(1))


def _sigmoid(x):
    return 1.0 / (1.0 + jnp.exp(-x))


def _gelu_tanh(x):
    return 0.5 * x * (1.0 + jnp.tanh(0.7978845608028654 * (x + 0.044715 * (x * x * x))))


def _rms_modulate(x, g, sc, sh):
    ms = jnp.mean(x * x, axis=-1, keepdims=True)
    return (x * lax.rsqrt(ms + EPS)) * g * (1.0 + sc) + sh


def _ada_kernel(c_ref, w_ref, b_ref, o_ref):
    c = c_ref[...]
    c_act = c * _sigmoid(c)
    o_ref[0] = jnp.dot(c_act.astype(BF16), w_ref[0].astype(BF16), preferred_element_type=F32) + b_ref[0]


def _ada_call(c, w_ada, b_ada):
    batch = c.shape[0]
    n_out = N_MOD * D_MODEL
    tn = n_out // 4
    return pl.pallas_call(
        _ada_kernel,
        grid=(DEPTH, n_out // tn),
        in_specs=[
            pl.BlockSpec((batch, D_MODEL), lambda l, n: (0, 0)),
            pl.BlockSpec((1, D_MODEL, tn), lambda l, n: (l, 0, n)),
            pl.BlockSpec((1, 1, tn), lambda l, n: (l, 0, n)),
        ],
        out_specs=pl.BlockSpec((1, batch, tn), lambda l, n: (l, 0, n)),
        out_shape=jax.ShapeDtypeStruct((DEPTH, batch, n_out), F32),
        compiler_params=pltpu.CompilerParams(
            dimension_semantics=("arbitrary", "arbitrary"), vmem_limit_bytes=VMEM_LIMIT),
        name="ada_mod",
    )(c, w_ada, b_ada.reshape(DEPTH, 1, n_out))


def _inproj_kernel(x_ref, g_ref, sc_ref, sh_ref, w_ref, lng_ref, lnb_ref,
                   u_ref, vln_ref, q0_ref, q1_ref, k_ref, v_ref, p_ref, gate_ref):
    h = _rms_modulate(x_ref[...], g_ref[...], sc_ref[0], sh_ref[0]).astype(BF16)
    bw = BRANCH_WIDTH

    nc = MXU_COLS
    halves = [slice(c * nc, (c + 1) * nc) for c in range(bw // nc)]

    def proj(col0):
        return jnp.dot(h, w_ref[:, col0:col0 + nc], preferred_element_type=F32)

    for c in range(N_BRANCH * D_MODEL // nc):
        gate_ref[:, c * nc:(c + 1) * nc] = _sigmoid(proj(6 * bw + c * nc)).astype(BF16)

    for c, cols in enumerate(halves):
        u_ref[:, cols] = _gelu_tanh(proj(c * nc)).astype(BF16)

    gv = [_gelu_tanh(proj(bw + c * nc)) for c in range(len(halves))]
    mu = sum(jnp.sum(g, axis=-1, keepdims=True) for g in gv) * (1.0 / bw)
    gc = [g - mu for g in gv]
    var = sum(jnp.sum(g * g, axis=-1, keepdims=True) for g in gc) * (1.0 / bw)
    inv = lax.rsqrt(var + EPS)
    for g, cols in zip(gc, halves):
        vln_ref[:, cols] = ((g * inv) * lng_ref[:, cols] + lnb_ref[:, cols]).astype(BF16)

    lane = lax.broadcasted_iota(jnp.int32, (x_ref.shape[0], nc), 1)
    first_of_pair = (lane & SB_HEAD_DIM) == 0
    for c, cols in enumerate(halves):
        q = proj(2 * bw + c * nc) * (SB_HEAD_DIM ** -0.5 * LOG2E)
        q0_ref[:, cols] = jnp.where(first_of_pair, q, 0.0).astype(BF16)
        q1_ref[:, cols] = jnp.where(first_of_pair, 0.0, q).astype(BF16)

    for n, ref in enumerate((k_ref, v_ref, p_ref)):
        for c, cols in enumerate(halves):
            ref[:, cols] = proj((3 + n) * bw + c * nc).astype(BF16)


def _inproj_call(x2d, rms_g, sc, sh, w_in, ln_g, ln_b, seq):
    m = x2d.shape[0]
    tm = TM_PROJ
    steps_per_seq = seq // tm
    row = lambda i: (i, 0)
    mod = lambda i: (i // steps_per_seq, 0, 0)
    piece = jax.ShapeDtypeStruct((m, BRANCH_WIDTH), BF16)
    return pl.pallas_call(
        _inproj_kernel,
        grid=(m // tm,),
        in_specs=[
            pl.BlockSpec((tm, D_MODEL), row),
            _resident((1, D_MODEL)),
            pl.BlockSpec((1, 1, D_MODEL), mod),
            pl.BlockSpec((1, 1, D_MODEL), mod),
            _resident((D_MODEL, IN_COLS)),
            _resident((1, BRANCH_WIDTH)),
            _resident((1, BRANCH_WIDTH)),
        ],
        out_specs=[pl.BlockSpec((tm, BRANCH_WIDTH), row)] * 7
        + [pl.BlockSpec((tm, N_BRANCH * D_MODEL), row)],
        out_shape=[piece] * 7 + [jax.ShapeDtypeStruct((m, N_BRANCH * D_MODEL), BF16)],
        compiler_params=pltpu.CompilerParams(
            dimension_semantics=("arbitrary",), vmem_limit_bytes=VMEM_LIMIT),
        name="in_proj",
    )(x2d, rms_g, sc, sh, w_in, ln_g, ln_b)


def _mix_kernel(x_ref, u_ref, vln_ref, q0_ref, q1_ref, k_ref, v_ref, p_ref, halo_ref,
                gate0_ref, gate2_ref, gate1_ref, gt_ref,
                ws_ref, bs_ref, pw_ref, ps_ref, wbr_ref, wout_ref, tri_ref, band_ref,
                o_ref, acc_ref, run_ref, br_scr, merged_scr, mergedb_scr):
    i = pl.program_id(1)
    n_tiles = pl.num_programs(1) - 1
    gate_refs = (gate0_ref, gate1_ref, gate2_ref)
    gw = GM_GROUP_WIDTH
    nc = MXU_COLS

    def gmlp_group(g):
        tr = lax.broadcasted_iota(jnp.int32, (GM_CHUNK, GM_CHUNK), 0)
        tc = lax.broadcasted_iota(jnp.int32, (GM_CHUNK, GM_CHUNK), 1)
        cols = slice(g * gw, (g + 1) * gw)
        wg = jnp.where(tc <= tr, ws_ref[g], 0.0).astype(BF16)
        for c in range(TQ // GM_CHUNK):
            rows = slice(c * GM_CHUNK, (c + 1) * GM_CHUNK)
            s = jnp.dot(wg, vln_ref[0, rows, cols], preferred_element_type=F32) + bs_ref[:, cols]
            br_scr[0, rows, cols] = (u_ref[0, rows, cols].astype(F32) * s).astype(BF16)

    def pool_diff(g):
        w = POOL_WINDOWS[g]
        cols = slice(g * POOL_GROUP_WIDTH, (g + 1) * POOL_GROUP_WIDTH)
        history = halo_ref[0, :, cols]
        history = jnp.where(i > 0, history, jnp.zeros_like(history))
        win = jnp.dot(band_ref[g], jnp.concatenate([history, p_ref[0, :, cols]], axis=0),
                      preferred_element_type=F32)
        cur = p_ref[0, :, cols].astype(F32)
        pos = i * TQ + lax.broadcasted_iota(jnp.int32, (TQ, 1), 0)
        count = jnp.minimum(pos + 1, w).astype(F32)
        br_scr[2, :, cols] = (win / count - cur).astype(BF16)

    def pool_map(g):
        cols = slice(g * POOL_GROUP_WIDTH, (g + 1) * POOL_GROUP_WIDTH)
        y = jnp.dot(br_scr[2, :, cols], pw_ref[g], preferred_element_type=F32) * ps_ref[:, cols]
        br_scr[2, :, cols] = y.astype(BF16)

    def gated_branch(n, cols):
        t = jnp.dot(br_scr[n], wbr_ref[n, :, cols], preferred_element_type=F32)
        return t * gate_refs[n][0, :, cols].astype(F32)

    def merge_local(c):
        cols = slice(c * nc, (c + 1) * nc)
        merged_scr[:, cols] = gated_branch(0, cols) + gated_branch(2, cols)

    def finish_merge(c):
        cols = slice(c * nc, (c + 1) * nc)
        mergedb_scr[:, cols] = (merged_scr[:, cols] + gated_branch(1, cols)).astype(BF16)

    def finish_output(c):
        cols = slice(c * nc, (c + 1) * nc)
        y = jnp.dot(mergedb_scr[...], wout_ref[:, cols], preferred_element_type=F32)
        o_ref[0, :, cols] = x_ref[0, :, cols] + gt_ref[0][:, cols] * y

    chunks = range(D_MODEL // nc)

    def side_work(finish_previous):
        local = []
        for g in range(GM_GROUPS):
            local += [functools.partial(gmlp_group, g), functools.partial(pool_diff, g)]
        maps = [functools.partial(pool_map, g) for g in range(len(POOL_WINDOWS))]
        merges = [functools.partial(merge_local, c) for c in chunks]
        if not finish_previous:
            return local + maps + merges
        return ([functools.partial(finish_merge, c) for c in chunks] + local
                + [functools.partial(finish_output, c) for c in chunks] + maps + merges)

    row_id = lax.broadcasted_iota(jnp.int32, (TQ, TK), 0)
    col_id = lax.broadcasted_iota(jnp.int32, (TQ, TK), 1)
    strictly_causal = col_id < row_id

    q_refs = (q0_ref, q1_ref)
    pair_cols = lambda head: slice((head // 2) * HEAD_PAIR, (head // 2 + 1) * HEAD_PAIR)

    def sweep(blocks, side_work=()):
        items = [piece + (head,) for piece in blocks for head in range(SB_HEADS)]
        side_work = list(side_work)
        n_steps = len(items) + 2 * PIPE_SKEW
        side_per_step = -(-len(side_work) // n_steps)
        log_beta, split, sums, weights, av, running = {}, {}, {}, {}, {}, {}

        def scores(n):
            j, row0, nrows, nkeys, masked, first, head = items[n]
            keys = pl.ds(pl.multiple_of(j * TK, TK), nkeys)
            z = lax.dot_general(q_refs[head % 2][0, row0:row0 + nrows, pair_cols(head)],
                                k_ref[0, keys, pair_cols(head)],
                                (((1,), (1,)), ((), ())), preferred_element_type=F32)
            neg_abs = lax.bitcast_convert_type(
                lax.bitcast_convert_type(z, jnp.int32) | jnp.int32(-2 ** 31), F32)
            lse = jnp.log(1.0 + jnp.exp2(neg_abs)) * LOG2E
            log_beta[n] = jnp.minimum(z, 0.0) - lse
            log_rest = log_beta[n] - z
            if masked:
                log_rest = jnp.where(strictly_causal[row0:row0 + nrows, :nkeys], log_rest, 0.0)
            split[n] = log_rest.astype(BF16)
            sums[n] = jnp.broadcast_to(jnp.sum(log_rest, axis=1, keepdims=True), (nrows, LANES))

        def suffix_and_weights(n):
            j, row0, nrows, nkeys, masked, first, head = items[n]
            suffix = jnp.dot(split.pop(n), tri_ref[:nkeys, :nkeys], preferred_element_type=F32)
            total = log_beta.pop(n) + suffix
            if first:
                running[n] = sums.pop(n)
            else:
                later = run_ref[head, row0:row0 + nrows]
                total = total + jnp.concatenate([later] * (nkeys // LANES), axis=1)
                running[n] = later + sums.pop(n)
            run_ref[head, row0:row0 + nrows] = running[n]
            a = jnp.exp2(total)
            if masked:
                a = jnp.where(strictly_causal[row0:row0 + nrows, :nkeys], a, 0.0)
            weights[n] = a.astype(BF16)

        def weighted_values(n):
            j, row0, nrows, nkeys, masked, first, head = items[n]
            keys = pl.ds(pl.multiple_of(j * TK, TK), nkeys)
            av[n] = jnp.dot(weights.pop(n), v_ref[0, keys, pair_cols(head)], preferred_element_type=F32)
            if head % 2:
                first_of_pair = lax.broadcasted_iota(jnp.int32, (nrows, HEAD_PAIR), 1) < SB_HEAD_DIM
                both = jnp.where(first_of_pair, av.pop(n - 1), av.pop(n))
                pair = head // 2
                rows = slice(row0, row0 + nrows)
                acc_ref[pair, rows] = both if first else acc_ref[pair, rows] + both

        for step in range(n_steps):
            if step < len(items):
                scores(step)
            if 0 <= step - PIPE_SKEW < len(items):
                suffix_and_weights(step - PIPE_SKEW)
            if step >= 2 * PIPE_SKEW:
                weighted_values(step - 2 * PIPE_SKEW)
            for _ in range(side_per_step):
                if side_work:
                    side_work.pop(0)()
        return [[running[b * SB_HEADS + head] for head in range(SB_HEADS)] for b in range(len(blocks))]

    def any_live(values):
        least_decayed = functools.reduce(jnp.maximum, values)
        return (jnp.max(least_decayed) > LOG2_F32_UNDERFLOW).astype(jnp.int32)

    half = TQ // 2
    diagonal = [(i, 0, half, half, True, True), (i, half, half, TK, True, True)]

    def first_tile():
        sweep(diagonal, side_work(False))
        return jnp.int32(0)

    def later_tile():
        run = sweep(diagonal + [(i - 1, 0, TQ, TK, False, False)], side_work(True))
        return any_live(run[2])

    def flush_last_tile():
        for c in chunks:
            finish_merge(c)
        for c in chunks:
            finish_output(c)
        return jnp.int32(0)

    live = lax.cond(i == 0, first_tile, lambda: lax.cond(i == n_tiles, flush_last_tile, later_tile))

    def earlier_block(carry):
        t, _ = carry
        return t + 1, any_live(sweep([(i - 1 - t, 0, TQ, TK, False, False)])[0])

    lax.while_loop(lambda carry: (carry[0] < i) & (carry[1] > 0), earlier_block, (jnp.int32(1), live))

    @pl.when(i < n_tiles)
    def _():
        for p in range(N_PAIRS):
            br_scr[1, :, p * HEAD_PAIR:(p + 1) * HEAD_PAIR] = acc_ref[p].astype(BF16)


def _mix_call(x, u, vln, q0, q1, k, v, p, gates, gt, ws, bs_full, pw, ps, wbr, wout, tri, band):
    batch, seq, _ = x.shape
    bw = BRANCH_WIDTH
    n_tiles = seq // TQ
    tile = lambda s: jnp.minimum(s, n_tiles - 1)
    done = lambda s: jnp.maximum(s - 1, 0)
    rows = lambda b, s: (b, tile(s), 0)
    whole = lambda b, s: (b, 0, 0)
    halo = lambda b, s: (b, jnp.maximum(tile(s) * (TQ // POOL_HALO) - 1, 0), 0)
    piece = pl.BlockSpec((1, TQ, bw), rows)
    gate_of = lambda n, which: pl.BlockSpec((1, TQ, D_MODEL), lambda b, s: (b, which(s), n))
    return pl.pallas_call(
        _mix_kernel,
        grid=(batch, n_tiles + 1),
        in_specs=[
            pl.BlockSpec((1, TQ, D_MODEL), lambda b, s: (b, done(s), 0)),
            piece, piece, piece, piece,
            pl.BlockSpec((1, seq, bw), whole),
            pl.BlockSpec((1, seq, bw), whole),
            piece,
            pl.BlockSpec((1, POOL_HALO, bw), halo),
            gate_of(0, tile), gate_of(2, tile), gate_of(1, done),
            pl.BlockSpec((1, 1, D_MODEL), whole),
            _resident((GM_GROUPS, GM_CHUNK, GM_CHUNK)),
            _resident((GM_CHUNK, bw)),
            _resident((len(POOL_WINDOWS), POOL_GROUP_WIDTH, POOL_GROUP_WIDTH)),
            _resident((1, bw)),
            _resident((N_BRANCH, bw, D_MODEL)),
            _resident((D_MODEL, D_MODEL)),
            _resident((TK, TK)),
            _resident((len(POOL_WINDOWS), TQ, POOL_HALO + TQ)),
        ],
        out_specs=pl.BlockSpec((1, TQ, D_MODEL), lambda b, s: (b, done(s), 0)),
        out_shape=jax.ShapeDtypeStruct(x.shape, F32),
        scratch_shapes=[
            pltpu.VMEM((N_PAIRS, TQ, HEAD_PAIR), F32),
            pltpu.VMEM((SB_HEADS, TQ, LANES), F32),
            pltpu.VMEM((N_BRANCH, TQ, bw), BF16),
            pltpu.VMEM((TQ, D_MODEL), F32),
            pltpu.VMEM((TQ, D_MODEL), BF16),
        ],
        compiler_params=pltpu.CompilerParams(
            dimension_semantics=("arbitrary", "arbitrary"), vmem_limit_bytes=VMEM_LIMIT),
        name="token_mix",
    )(x, u, vln, q0, q1, k, v, p, p, gates, gates, gates, gt, ws, bs_full, pw, ps, wbr, wout, tri, band)


def _ffn_kernel(x_ref, g_ref, sc_ref, sh_ref, gt_ref, win_ref, wout_ref, fg_ref, o_ref, hid_scr, *, final):
    x = x_ref[...]
    h = _rms_modulate(x, g_ref[...], sc_ref[0], sh_ref[0]).astype(BF16)
    for c in range(D_FF // FF_CHUNK):
        c0 = c * FF_CHUNK
        f_gate = jnp.dot(h, win_ref[:, c0:c0 + FF_CHUNK], preferred_element_type=F32)
        f_up = jnp.dot(h, win_ref[:, D_FF + c0:D_FF + c0 + FF_CHUNK], preferred_element_type=F32)
        hid_scr[:, c0:c0 + FF_CHUNK] = (f_gate * _sigmoid(f_gate) * f_up).astype(BF16)
    y = jnp.dot(hid_scr[...], wout_ref[...], preferred_element_type=F32)
    xn = x + gt_ref[0] * y
    if final:
        ms = jnp.mean(xn * xn, axis=-1, keepdims=True)
        xn = (xn * lax.rsqrt(ms + EPS)) * fg_ref[...]
    o_ref[...] = xn


def _ffn_call(x2d, rms_g, sc, sh, gt, w_in, w_out, final_g, seq, final):
    m = x2d.shape[0]
    tm = TM_PROJ
    steps_per_seq = seq // tm
    row = lambda i: (i, 0)
    mod = lambda i: (i // steps_per_seq, 0, 0)
    return pl.pallas_call(
        functools.partial(_ffn_kernel, final=final),
        grid=(m // tm,),
        in_specs=[
            pl.BlockSpec((tm, D_MODEL), row),
            _resident((1, D_MODEL)),
            pl.BlockSpec((1, 1, D_MODEL), mod),
            pl.BlockSpec((1, 1, D_MODEL), mod),
            pl.BlockSpec((1, 1, D_MODEL), mod),
            _resident((D_MODEL, 2 * D_FF)),
            _resident((D_FF, D_MODEL)),
            _resident((1, D_MODEL)),
        ],
        out_specs=pl.BlockSpec((tm, D_MODEL), row),
        out_shape=jax.ShapeDtypeStruct(x2d.shape, F32),
        scratch_shapes=[pltpu.VMEM((tm, D_FF), BF16)],
        compiler_params=pltpu.CompilerParams(
            dimension_semantics=("arbitrary",), vmem_limit_bytes=VMEM_LIMIT),
        name="ffn_final" if final else "ffn",
    )(x2d, rms_g, sc, sh, gt, w_in, w_out, final_g)


def _suffix_matrix():
    j = lax.broadcasted_iota(jnp.int32, (TK, TK), 0)
    s = lax.broadcasted_iota(jnp.int32, (TK, TK), 1)
    return (j > s).astype(BF16)


def _window_matrices():
    t = lax.broadcasted_iota(jnp.int32, (TQ, POOL_HALO + TQ), 0)
    r = lax.broadcasted_iota(jnp.int32, (TQ, POOL_HALO + TQ), 1)
    back = t + POOL_HALO - r
    return jnp.stack([((back >= 0) & (back < w)).astype(BF16) for w in POOL_WINDOWS])


def kernel(x, c, rms_g1, rms_g2, w_ada, b_ada, w_in, gm_ln_g, gm_ln_b, gm_w_spatial, gm_b_spatial,
           pool_w, pool_scale, w_branch, w_out, w_ffn_in, w_ffn_out, final_g):
    batch, seq, d = x.shape
    assert d == D_MODEL and seq % TM_PROJ == 0 and seq % TQ == 0 and TQ == TK and TQ % GM_CHUNK == 0
    m = batch * seq

    mod = _ada_call(c, w_ada, b_ada)
    mod = mod.reshape(DEPTH, batch, N_MOD, 1, D_MODEL)
    tri = _suffix_matrix()
    band = _window_matrices()
    final_row = final_g.reshape(1, D_MODEL)

    for l in range(DEPTH):
        sh1, sc1, gt1, sh2, sc2, gt2 = [mod[l, :, n] for n in range(N_MOD)]
        u, vln, q0, q1, k, v, p, gates = _inproj_call(
            x.reshape(m, D_MODEL), rms_g1[l].reshape(1, D_MODEL), sc1, sh1, w_in[l].astype(BF16),
            gm_ln_g[l].reshape(1, BRANCH_WIDTH), gm_ln_b[l].reshape(1, BRANCH_WIDTH), seq)
        to_seq = lambda a: a.reshape(batch, seq, a.shape[-1])
        bs_full = jnp.repeat(gm_b_spatial[l].T, GM_GROUP_WIDTH, axis=1)
        x = _mix_call(
            x, to_seq(u), to_seq(vln), to_seq(q0), to_seq(q1), to_seq(k), to_seq(v), to_seq(p),
            to_seq(gates), gt1, gm_w_spatial[l], bs_full, pool_w[l].astype(BF16),
            pool_scale[l].reshape(1, BRANCH_WIDTH), w_branch[l].astype(BF16), w_out[l].astype(BF16), tri,
            band)
        x = _ffn_call(
            x.reshape(m, D_MODEL), rms_g2[l].reshape(1, D_MODEL), sc2, sh2, gt2,
            w_ffn_in[l].astype(BF16), w_ffn_out[l].astype(BF16), final_row, seq,
            final=(l == DEPTH - 1)).reshape(batch, seq, D_MODEL)
    return x
```

```python
import functools
import math

import jax
import jax.numpy as jnp
from jax import lax
from jax.experimental import pallas as pl
from jax.experimental.pallas import tpu as pltpu

F32 = jnp.float32
BF16 = jnp.bfloat16

D_MODEL = 1024
DEPTH = 4
BRANCH_WIDTH = D_MODEL // 2
N_BRANCH = 3
GM_CHUNK = 128
GM_GROUPS = 4
GM_GROUP_WIDTH = BRANCH_WIDTH // GM_GROUPS
SB_HEAD_DIM = 64
SB_HEADS = BRANCH_WIDTH // SB_HEAD_DIM
POOL_WINDOWS = (2, 4, 8, 16)
POOL_GROUP_WIDTH = BRANCH_WIDTH // len(POOL_WINDOWS)
D_FF = -(-8 * D_MODEL // (3 * 256)) * 256
N_MOD = 6
EPS = 1e-6
IN_COLS = 6 * BRANCH_WIDTH + N_BRANCH * D_MODEL

LANES = 128
MXU_COLS = 256
HEAD_PAIR = 2 * SB_HEAD_DIM
N_PAIRS = SB_HEADS // 2
POOL_HALO = 16
LOG2E = 1.4426950408889634
LOG2_F32_UNDERFLOW = -152.0
VMEM_LIMIT = 56 * 1024 * 1024

TM_PROJ = 1024
TQ = 256
TK = 256
TILE_SKEW = 4
LOOP_SKEW = 2
FF_CHUNK = 256


def _resident(shape):
    nd = len(shape)
    return pl.BlockSpec(shape, lambda *_: (0,) * nd, pipeline_mode=pl.Buffered(1))


def _sigmoid(x):
    return 0.5 * jnp.tanh(0.5 * x) + 0.5


def _gelu_tanh(x):
    c = 0.7978845608028654
    half = 0.5 * x
    return half + half * jnp.tanh(x * (c + (c * 0.044715) * (x * x)))


def _rms_modulate(x, g, sc, sh):
    ms = jnp.mean(x * x, axis=-1, keepdims=True)
    return (x * lax.rsqrt(ms + EPS)) * g * (1.0 + sc) + sh


def _ada_kernel(c_ref, w_ref, b_ref, o_ref):
    c = c_ref[...]
    c_act = c * _sigmoid(c)
    o_ref[0] = jnp.dot(c_act.astype(BF16), w_ref[0].astype(BF16), preferred_element_type=F32) + b_ref[0]


def _ada_call(c, w_ada, b_ada):
    batch = c.shape[0]
    n_out = N_MOD * D_MODEL
    tn = n_out // 4
    return pl.pallas_call(
        _ada_kernel,
        grid=(DEPTH, n_out // tn),
        in_specs=[
            pl.BlockSpec((batch, D_MODEL), lambda l, n: (0, 0)),
            pl.BlockSpec((1, D_MODEL, tn), lambda l, n: (l, 0, n)),
            pl.BlockSpec((1, 1, tn), lambda l, n: (l, 0, n)),
        ],
        out_specs=pl.BlockSpec((1, batch, tn), lambda l, n: (l, 0, n)),
        out_shape=jax.ShapeDtypeStruct((DEPTH, batch, n_out), F32),
        compiler_params=pltpu.CompilerParams(
            dimension_semantics=("arbitrary", "arbitrary"), vmem_limit_bytes=VMEM_LIMIT),
        name="ada_mod",
    )(c, w_ada, b_ada.reshape(DEPTH, 1, n_out))


def _inproj_kernel(x_ref, g_ref, sc_ref, sh_ref, w_ref, lng_ref, lnb_ref,
                   u_ref, vln_ref, q0_ref, q1_ref, k_ref, v_ref, p_ref, gate_ref):
    h = _rms_modulate(x_ref[...], g_ref[...], sc_ref[0], sh_ref[0]).astype(BF16)
    bw = BRANCH_WIDTH

    nc = MXU_COLS
    halves = [slice(c * nc, (c + 1) * nc) for c in range(bw // nc)]

    def proj(col0):
        return jnp.dot(h, w_ref[:, col0:col0 + nc], preferred_element_type=F32)

    for c in range(N_BRANCH * D_MODEL // nc):
        gate_ref[:, c * nc:(c + 1) * nc] = _sigmoid(proj(6 * bw + c * nc)).astype(BF16)

    for c, cols in enumerate(halves):
        u_ref[:, cols] = _gelu_tanh(proj(c * nc)).astype(BF16)

    gv = [_gelu_tanh(proj(bw + c * nc)) for c in range(len(halves))]
    mu = sum(jnp.sum(g, axis=-1, keepdims=True) for g in gv) * (1.0 / bw)
    gc = [g - mu for g in gv]
    var = sum(jnp.sum(g * g, axis=-1, keepdims=True) for g in gc) * (1.0 / bw)
    inv = lax.rsqrt(var + EPS)
    for g, cols in zip(gc, halves):
        vln_ref[:, cols] = ((g * inv) * lng_ref[:, cols] + lnb_ref[:, cols]).astype(BF16)

    for n, ref in enumerate((k_ref, v_ref, p_ref)):
        for c, cols in enumerate(halves):
            ref[:, cols] = proj((3 + n) * bw + c * nc).astype(BF16)

    lane = lax.broadcasted_iota(jnp.int32, (x_ref.shape[0], nc), 1)
    first_of_pair = (lane & SB_HEAD_DIM) == 0
    for c, cols in enumerate(halves):
        q = proj(2 * bw + c * nc) * (SB_HEAD_DIM ** -0.5 * LOG2E)
        q0_ref[:, cols] = jnp.where(first_of_pair, q, 0.0).astype(BF16)
        q1_ref[:, cols] = jnp.where(first_of_pair, 0.0, q).astype(BF16)


def _inproj_call(x2d, rms_g, sc, sh, w_in, ln_g, ln_b, seq):
    m = x2d.shape[0]
    tm = TM_PROJ
    steps_per_seq = seq // tm
    row = lambda i: (i, 0)
    mod = lambda i: (i // steps_per_seq, 0, 0)
    piece = jax.ShapeDtypeStruct((m, BRANCH_WIDTH), BF16)
    return pl.pallas_call(
        _inproj_kernel,
        grid=(m // tm,),
        in_specs=[
            pl.BlockSpec((tm, D_MODEL), row),
            _resident((1, D_MODEL)),
            pl.BlockSpec((1, 1, D_MODEL), mod),
            pl.BlockSpec((1, 1, D_MODEL), mod),
            _resident((D_MODEL, IN_COLS)),
            _resident((1, BRANCH_WIDTH)),
            _resident((1, BRANCH_WIDTH)),
        ],
        out_specs=[pl.BlockSpec((tm, BRANCH_WIDTH), row)] * 7
        + [pl.BlockSpec((tm, N_BRANCH * D_MODEL), row)],
        out_shape=[piece] * 7 + [jax.ShapeDtypeStruct((m, N_BRANCH * D_MODEL), BF16)],
        compiler_params=pltpu.CompilerParams(
            dimension_semantics=("arbitrary",), vmem_limit_bytes=VMEM_LIMIT),
        name="in_proj",
    )(x2d, rms_g, sc, sh, w_in, ln_g, ln_b)


def _mix_kernel(x_ref, u_ref, vln_ref, q0_ref, q1_ref, k_ref, v_ref, p_ref, halo_ref,
                gate0_ref, gate2_ref, gate1_ref, gt_ref,
                ws_ref, bs_ref, pw_ref, ps_ref, wbr_ref, wout_ref, tri_ref, band_ref,
                o_ref, acc_ref, run_ref, br_scr, merged_scr, mergedb_scr):
    i = pl.program_id(1)
    n_tiles = pl.num_programs(1) - 1
    gate_refs = (gate0_ref, gate1_ref, gate2_ref)
    gw = GM_GROUP_WIDTH
    nc = MXU_COLS

    def gmlp_group(g):
        tr = lax.broadcasted_iota(jnp.int32, (GM_CHUNK, GM_CHUNK), 0)
        tc = lax.broadcasted_iota(jnp.int32, (GM_CHUNK, GM_CHUNK), 1)
        cols = slice(g * gw, (g + 1) * gw)
        wg = jnp.where(tc <= tr, ws_ref[g], 0.0).astype(BF16)
        for c in range(TQ // GM_CHUNK):
            rows = slice(c * GM_CHUNK, (c + 1) * GM_CHUNK)
            s = jnp.dot(wg, vln_ref[0, rows, cols], preferred_element_type=F32) + bs_ref[:, cols]
            br_scr[0, rows, cols] = (u_ref[0, rows, cols].astype(F32) * s).astype(BF16)

    def pool_diff(g):
        w = POOL_WINDOWS[g]
        cols = slice(g * POOL_GROUP_WIDTH, (g + 1) * POOL_GROUP_WIDTH)
        history = halo_ref[0, :, cols]
        history = jnp.where(i > 0, history, jnp.zeros_like(history))
        win = jnp.dot(band_ref[g], jnp.concatenate([history, p_ref[0, :, cols]], axis=0),
                      preferred_element_type=F32)
        cur = p_ref[0, :, cols].astype(F32)
        pos = i * TQ + lax.broadcasted_iota(jnp.int32, (TQ, 1), 0)
        count = jnp.minimum(pos + 1, w).astype(F32)
        br_scr[2, :, cols] = (win / count - cur).astype(BF16)

    def pool_map(g):
        cols = slice(g * POOL_GROUP_WIDTH, (g + 1) * POOL_GROUP_WIDTH)
        y = jnp.dot(br_scr[2, :, cols], pw_ref[g], preferred_element_type=F32) * ps_ref[:, cols]
        br_scr[2, :, cols] = y.astype(BF16)

    def gated_branch(n, cols):
        t = jnp.dot(br_scr[n], wbr_ref[n, :, cols], preferred_element_type=F32)
        return t * gate_refs[n][0, :, cols].astype(F32)

    def merge_local(c):
        cols = slice(c * nc, (c + 1) * nc)
        merged_scr[:, cols] = gated_branch(0, cols) + gated_branch(2, cols)

    def finish_merge(c):
        cols = slice(c * nc, (c + 1) * nc)
        mergedb_scr[:, cols] = (merged_scr[:, cols] + gated_branch(1, cols)).astype(BF16)

    def finish_output(c):
        cols = slice(c * nc, (c + 1) * nc)
        y = jnp.dot(mergedb_scr[...], wout_ref[:, cols], preferred_element_type=F32)
        o_ref[0, :, cols] = x_ref[0, :, cols] + gt_ref[0][:, cols] * y

    chunks = range(D_MODEL // nc)

    def side_work(finish_previous):
        local = []
        for g in range(GM_GROUPS):
            local += [functools.partial(gmlp_group, g), functools.partial(pool_diff, g)]
        maps = [functools.partial(pool_map, g) for g in range(len(POOL_WINDOWS))]
        merges = [functools.partial(merge_local, c) for c in chunks]
        if not finish_previous:
            return local + maps + merges
        return ([functools.partial(finish_merge, c) for c in chunks] + local
                + [functools.partial(finish_output, c) for c in chunks] + maps + merges)

    row_id = lax.broadcasted_iota(jnp.int32, (TQ, TK), 0)
    col_id = lax.broadcasted_iota(jnp.int32, (TQ, TK), 1)
    strictly_causal = col_id < row_id

    q_refs = (q0_ref, q1_ref)
    pair_cols = lambda head: slice((head // 2) * HEAD_PAIR, (head // 2 + 1) * HEAD_PAIR)

    def sweep(blocks, skew, side_work=()):
        items = [piece + (head,) for piece in blocks for head in range(SB_HEADS)]
        side_work = list(side_work)
        n_steps = len(items) + 2 * skew
        side_per_step = -(-len(side_work) // n_steps)
        log_beta, split, sums, weights, av, running = {}, {}, {}, {}, {}, {}

        def scores(n):
            j, row0, nrows, nkeys, masked, first, head = items[n]
            keys = pl.ds(pl.multiple_of(j * TK, TK), nkeys)
            z = lax.dot_general(q_refs[head % 2][0, row0:row0 + nrows, pair_cols(head)],
                                k_ref[0, keys, pair_cols(head)],
                                (((1,), (1,)), ((), ())), preferred_element_type=F32)
            neg_abs = lax.bitcast_convert_type(
                lax.bitcast_convert_type(z, jnp.int32) | jnp.int32(-2 ** 31), F32)
            lse = jnp.log(1.0 + jnp.exp2(neg_abs)) * LOG2E
            log_beta[n] = jnp.minimum(z, 0.0) - lse
            log_rest = log_beta[n] - z
            if masked:
                log_rest = jnp.where(strictly_causal[row0:row0 + nrows, :nkeys], log_rest, 0.0)
            split[n] = log_rest.astype(BF16)
            sums[n] = jnp.broadcast_to(jnp.sum(log_rest, axis=1, keepdims=True), (nrows, LANES))

        def suffix_and_weights(n):
            j, row0, nrows, nkeys, masked, first, head = items[n]
            suffix = jnp.dot(split.pop(n), tri_ref[:nkeys, :nkeys], preferred_element_type=F32)
            total = log_beta.pop(n) + suffix
            if first:
                running[n] = sums.pop(n)
            else:
                later = run_ref[head, row0:row0 + nrows]
                total = total + jnp.concatenate([later] * (nkeys // LANES), axis=1)
                running[n] = later + sums.pop(n)
            run_ref[head, row0:row0 + nrows] = running[n]
            a = jnp.exp2(total)
            if masked:
                a = jnp.where(strictly_causal[row0:row0 + nrows, :nkeys], a, 0.0)
            weights[n] = a.astype(BF16)

        def weighted_values(n):
            j, row0, nrows, nkeys, masked, first, head = items[n]
            keys = pl.ds(pl.multiple_of(j * TK, TK), nkeys)
            av[n] = jnp.dot(weights.pop(n), v_ref[0, keys, pair_cols(head)], preferred_element_type=F32)
            if head % 2:
                first_of_pair = lax.broadcasted_iota(jnp.int32, (nrows, HEAD_PAIR), 1) < SB_HEAD_DIM
                both = jnp.where(first_of_pair, av.pop(n - 1), av.pop(n))
                pair = head // 2
                rows = slice(row0, row0 + nrows)
                acc_ref[pair, rows] = both if first else acc_ref[pair, rows] + both

        for step in range(n_steps):
            if step < len(items):
                scores(step)
            if 0 <= step - skew < len(items):
                suffix_and_weights(step - skew)
            if step >= 2 * skew:
                weighted_values(step - 2 * skew)
            for _ in range(side_per_step):
                if side_work:
                    side_work.pop(0)()
        return [[running[b * SB_HEADS + head] for head in range(SB_HEADS)] for b in range(len(blocks))]

    def any_live(values):
        least_decayed = functools.reduce(jnp.maximum, values)
        return (jnp.max(least_decayed) > LOG2_F32_UNDERFLOW).astype(jnp.int32)

    half = TQ // 2
    diagonal = [(i, 0, half, half, True, True), (i, half, half, TK, True, True)]

    def first_tile():
        sweep(diagonal, TILE_SKEW, side_work(False))
        return jnp.int32(0)

    def later_tile():
        run = sweep(diagonal + [(i - 1, 0, TQ, TK, False, False)], TILE_SKEW, side_work(True))
        return any_live(run[2])

    def flush_last_tile():
        for c in chunks:
            finish_merge(c)
        for c in chunks:
            finish_output(c)
        return jnp.int32(0)

    live = lax.cond(i == 0, first_tile, lambda: lax.cond(i == n_tiles, flush_last_tile, later_tile))

    def earlier_block(carry):
        t, _ = carry
        return t + 1, any_live(sweep([(i - 1 - t, 0, TQ, TK, False, False)], LOOP_SKEW)[0])

    lax.while_loop(lambda carry: (carry[0] < i) & (carry[1] > 0), earlier_block, (jnp.int32(1), live))

    @pl.when(i < n_tiles)
    def _():
        for p in range(N_PAIRS):
            br_scr[1, :, p * HEAD_PAIR:(p + 1) * HEAD_PAIR] = acc_ref[p].astype(BF16)


def _mix_call(x, u, vln, q0, q1, k, v, p, gates, gt, ws, bs_full, pw, ps, wbr, wout, tri, band):
    batch, seq, _ = x.shape
    bw = BRANCH_WIDTH
    n_tiles = seq // TQ
    tile = lambda s: jnp.minimum(s, n_tiles - 1)
    done = lambda s: jnp.maximum(s - 1, 0)
    rows = lambda b, s: (b, tile(s), 0)
    whole = lambda b, s: (b, 0, 0)
    halo = lambda b, s: (b, jnp.maximum(tile(s) * (TQ // POOL_HALO) - 1, 0), 0)
    piece = pl.BlockSpec((1, TQ, bw), rows)
    gate_of = lambda n, which: pl.BlockSpec((1, TQ, D_MODEL), lambda b, s: (b, which(s), n))
    return pl.pallas_call(
        _mix_kernel,
        grid=(batch, n_tiles + 1),
        in_specs=[
            pl.BlockSpec((1, TQ, D_MODEL), lambda b, s: (b, done(s), 0)),
            piece, piece, piece, piece,
            pl.BlockSpec((1, seq, bw), whole),
            pl.BlockSpec((1, seq, bw), whole),
            piece,
            pl.BlockSpec((1, POOL_HALO, bw), halo),
            gate_of(0, tile), gate_of(2, tile), gate_of(1, done),
            pl.BlockSpec((1, 1, D_MODEL), whole),
            _resident((GM_GROUPS, GM_CHUNK, GM_CHUNK)),
            _resident((GM_CHUNK, bw)),
            _resident((len(POOL_WINDOWS), POOL_GROUP_WIDTH, POOL_GROUP_WIDTH)),
            _resident((1, bw)),
            _resident((N_BRANCH, bw, D_MODEL)),
            _resident((D_MODEL, D_MODEL)),
            _resident((TK, TK)),
            _resident((len(POOL_WINDOWS), TQ, POOL_HALO + TQ)),
        ],
        out_specs=pl.BlockSpec((1, TQ, D_MODEL), lambda b, s: (b, done(s), 0)),
        out_shape=jax.ShapeDtypeStruct(x.shape, F32),
        scratch_shapes=[
            pltpu.VMEM((N_PAIRS, TQ, HEAD_PAIR), F32),
            pltpu.VMEM((SB_HEADS, TQ, LANES), F32),
            pltpu.VMEM((N_BRANCH, TQ, bw), BF16),
            pltpu.VMEM((TQ, D_MODEL), F32),
            pltpu.VMEM((TQ, D_MODEL), BF16),
        ],
        compiler_params=pltpu.CompilerParams(
            dimension_semantics=("arbitrary", "arbitrary"), vmem_limit_bytes=VMEM_LIMIT),
        name="token_mix",
    )(x, u, vln, q0, q1, k, v, p, p, gates, gates, gates, gt, ws, bs_full, pw, ps, wbr, wout, tri, band)


def _ffn_kernel(x_ref, g_ref, sc_ref, sh_ref, gt_ref, win_ref, wout_ref, fg_ref, o_ref, hid_scr, *, final):
    x = x_ref[...]
    h = _rms_modulate(x, g_ref[...], sc_ref[0], sh_ref[0]).astype(BF16)
    for c in range(D_FF // FF_CHUNK):
        c0 = c * FF_CHUNK
        f_gate = jnp.dot(h, win_ref[:, c0:c0 + FF_CHUNK], preferred_element_type=F32)
        f_up = jnp.dot(h, win_ref[:, D_FF + c0:D_FF + c0 + FF_CHUNK], preferred_element_type=F32)
        hid_scr[:, c0:c0 + FF_CHUNK] = (f_gate * _sigmoid(f_gate) * f_up).astype(BF16)
    y = jnp.dot(hid_scr[...], wout_ref[...], preferred_element_type=F32)
    xn = x + gt_ref[0] * y
    if final:
        ms = jnp.mean(xn * xn, axis=-1, keepdims=True)
        xn = (xn * lax.rsqrt(ms + EPS)) * fg_ref[...]
    o_ref[...] = xn


def _ffn_call(x2d, rms_g, sc, sh, gt, w_in, w_out, final_g, seq, final):
    m = x2d.shape[0]
    tm = TM_PROJ
    steps_per_seq = seq // tm
    row = lambda i: (i, 0)
    mod = lambda i: (i // steps_per_seq, 0, 0)
    return pl.pallas_call(
        functools.partial(_ffn_kernel, final=final),
        grid=(m // tm,),
        in_specs=[
            pl.BlockSpec((tm, D_MODEL), row),
            _resident((1, D_MODEL)),
            pl.BlockSpec((1, 1, D_MODEL), mod),
            pl.BlockSpec((1, 1, D_MODEL), mod),
            pl.BlockSpec((1, 1, D_MODEL), mod),
            _resident((D_MODEL, 2 * D_FF)),
            _resident((D_FF, D_MODEL)),
            _resident((1, D_MODEL)),
        ],
        out_specs=pl.BlockSpec((tm, D_MODEL), row),
        out_shape=jax.ShapeDtypeStruct(x2d.shape, F32),
        scratch_shapes=[pltpu.VMEM((tm, D_FF), BF16)],
        compiler_params=pltpu.CompilerParams(
            dimension_semantics=("arbitrary",), vmem_limit_bytes=VMEM_LIMIT),
        name="ffn_final" if final else "ffn",
    )(x2d, rms_g, sc, sh, gt, w_in, w_out, final_g)


def _suffix_matrix():
    j = lax.broadcasted_iota(jnp.int32, (TK, TK), 0)
    s = lax.broadcasted_iota(jnp.int32, (TK, TK), 1)
    return (j > s).astype(BF16)


def _window_matrices():
    t = lax.broadcasted_iota(jnp.int32, (TQ, POOL_HALO + TQ), 0)
    r = lax.broadcasted_iota(jnp.int32, (TQ, POOL_HALO + TQ), 1)
    back = t + POOL_HALO - r
    return jnp.stack([((back >= 0) & (back < w)).astype(BF16) for w in POOL_WINDOWS])


def kernel(x, c, rms_g1, rms_g2, w_ada, b_ada, w_in, gm_ln_g, gm_ln_b, gm_w_spatial, gm_b_spatial,
           pool_w, pool_scale, w_branch, w_out, w_ffn_in, w_ffn_out, final_g):
    batch, seq, d = x.shape
    assert d == D_MODEL and seq % TM_PROJ == 0 and seq % TQ == 0 and TQ == TK and TQ % GM_CHUNK == 0
    m = batch * seq

    mod = _ada_call(c, w_ada, b_ada)
    mod = mod.reshape(DEPTH, batch, N_MOD, 1, D_MODEL)
    tri = _suffix_matrix()
    band = _window_matrices()
    final_row = final_g.reshape(1, D_MODEL)

    for l in range(DEPTH):
        sh1, sc1, gt1, sh2, sc2, gt2 = [mod[l, :, n] for n in range(N_MOD)]
        u, vln, q0, q1, k, v, p, gates = _inproj_call(
            x.reshape(m, D_MODEL), rms_g1[l].reshape(1, D_MODEL), sc1, sh1, w_in[l].astype(BF16),
            gm_ln_g[l].reshape(1, BRANCH_WIDTH), gm_ln_b[l].reshape(1, BRANCH_WIDTH), seq)
        to_seq = lambda a: a.reshape(batch, seq, a.shape[-1])
        bs_full = jnp.repeat(gm_b_spatial[l].T, GM_GROUP_WIDTH, axis=1)
        x = _mix_call(
            x, to_seq(u), to_seq(vln), to_seq(q0), to_seq(q1), to_seq(k), to_seq(v), to_seq(p),
            to_seq(gates), gt1, gm_w_spatial[l], bs_full, pool_w[l].astype(BF16),
            pool_scale[l].reshape(1, BRANCH_WIDTH), w_branch[l].astype(BF16), w_out[l].astype(BF16), tri,
            band)
        x = _ffn_call(
            x.reshape(m, D_MODEL), rms_g2[l].reshape(1, D_MODEL), sc2, sh2, gt2,
            w_ffn_in[l].astype(BF16), w_ffn_out[l].astype(BF16), final_row, seq,
            final=(l == DEPTH - 1)).reshape(batch, seq, D_MODEL)
    return x
```

```python
import functools

import jax
import jax.numpy as jnp
from jax import lax
from jax.experimental import pallas as pl
from jax.experimental.pallas import tpu as pltpu

F32 = jnp.float32
BF16 = jnp.bfloat16

D_MODEL = 1024
DEPTH = 4
BRANCH_WIDTH = D_MODEL // 2
N_BRANCH = 3
GM_CHUNK = 128
GM_GROUPS = 4
GM_GROUP_WIDTH = BRANCH_WIDTH // GM_GROUPS
SB_HEAD_DIM = 64
SB_HEADS = BRANCH_WIDTH // SB_HEAD_DIM
POOL_WINDOWS = (2, 4, 8, 16)
POOL_GROUP_WIDTH = BRANCH_WIDTH // len(POOL_WINDOWS)
D_FF = -(-8 * D_MODEL // (3 * 256)) * 256
N_MOD = 6
EPS = 1e-6
IN_COLS = 6 * BRANCH_WIDTH + N_BRANCH * D_MODEL

LANES = 128
MXU_COLS = 256
HEAD_PAIR = 2 * SB_HEAD_DIM
N_PAIRS = SB_HEADS // 2
POOL_HALO = 16
LOG2E = 1.4426950408889634
LOG2_F32_UNDERFLOW = -152.0
VMEM_LIMIT = 56 * 1024 * 1024

TM_PROJ = 1024
TQ = 256
TK = 256
TILE_SKEW = 4
LOOP_SKEW = 2
FF_CHUNK = 256


def _resident(shape):
    nd = len(shape)
    return pl.BlockSpec(shape, lambda *_: (0,) * nd, pipeline_mode=pl.Buffered(1))


def _sigmoid(x):
    return 0.5 * jnp.tanh(0.5 * x) + 0.5


def _silu(x):
    half = 0.5 * x
    return half + half * jnp.tanh(half)


def _gelu_tanh(x):
    c = 0.7978845608028654
    half = 0.5 * x
    return half + half * jnp.tanh(x * (c + (c * 0.044715) * (x * x)))


def _rms_modulate(x, g, sc, sh):
    ms = jnp.mean(x * x, axis=-1, keepdims=True)
    return (x * lax.rsqrt(ms + EPS)) * (g * (1.0 + sc)) + sh


def _ada_kernel(c_ref, w_ref, b_ref, o_ref):
    c = c_ref[...]
    c_act = _silu(c)
    o_ref[0] = jnp.dot(c_act.astype(BF16), w_ref[0].astype(BF16), preferred_element_type=F32) + b_ref[0]


def _ada_call(c, w_ada, b_ada):
    batch = c.shape[0]
    n_out = N_MOD * D_MODEL
    tn = n_out // 4
    return pl.pallas_call(
        _ada_kernel,
        grid=(DEPTH, n_out // tn),
        in_specs=[
            pl.BlockSpec((batch, D_MODEL), lambda l, n: (0, 0)),
            pl.BlockSpec((1, D_MODEL, tn), lambda l, n: (l, 0, n)),
            pl.BlockSpec((1, 1, tn), lambda l, n: (l, 0, n)),
        ],
        out_specs=pl.BlockSpec((1, batch, tn), lambda l, n: (l, 0, n)),
        out_shape=jax.ShapeDtypeStruct((DEPTH, batch, n_out), F32),
        compiler_params=pltpu.CompilerParams(
            dimension_semantics=("arbitrary", "arbitrary"), vmem_limit_bytes=VMEM_LIMIT),
        name="ada_mod",
    )(c, w_ada, b_ada.reshape(DEPTH, 1, n_out))


def _inproj_kernel(x_ref, g_ref, sc_ref, sh_ref, w_ref, lng_ref, lnb_ref,
                   u_ref, vln_ref, q0_ref, q1_ref, k_ref, v_ref, p_ref, gate_ref):
    h = _rms_modulate(x_ref[...], g_ref[...], sc_ref[0], sh_ref[0]).astype(BF16)
    bw = BRANCH_WIDTH

    nc = MXU_COLS
    halves = [slice(c * nc, (c + 1) * nc) for c in range(bw // nc)]

    def proj(col0):
        return jnp.dot(h, w_ref[:, col0:col0 + nc], preferred_element_type=F32)

    for c in range(N_BRANCH * D_MODEL // nc):
        gate_ref[:, c * nc:(c + 1) * nc] = _sigmoid(proj(6 * bw + c * nc)).astype(BF16)

    for c, cols in enumerate(halves):
        u_ref[:, cols] = _gelu_tanh(proj(c * nc)).astype(BF16)

    gv = [_gelu_tanh(proj(bw + c * nc)) for c in range(len(halves))]
    mu = sum(jnp.sum(g, axis=-1, keepdims=True) for g in gv) * (1.0 / bw)
    gc = [g - mu for g in gv]
    var = sum(jnp.sum(g * g, axis=-1, keepdims=True) for g in gc) * (1.0 / bw)
    inv = lax.rsqrt(var + EPS)
    for g, cols in zip(gc, halves):
        vln_ref[:, cols] = ((g * inv) * lng_ref[:, cols] + lnb_ref[:, cols]).astype(BF16)

    for n, ref in enumerate((k_ref, v_ref, p_ref)):
        for c, cols in enumerate(halves):
            ref[:, cols] = proj((3 + n) * bw + c * nc).astype(BF16)

    lane = lax.broadcasted_iota(jnp.int32, (1, nc), 1)
    first_of_pair = (lane & SB_HEAD_DIM) == 0
    score_scale = SB_HEAD_DIM ** -0.5 * LOG2E
    scale0 = jnp.where(first_of_pair, score_scale, 0.0)
    scale1 = jnp.where(first_of_pair, 0.0, score_scale)
    for c, cols in enumerate(halves):
        q = proj(2 * bw + c * nc)
        q0_ref[:, cols] = (q * scale0).astype(BF16)
        q1_ref[:, cols] = (q * scale1).astype(BF16)


def _inproj_call(x2d, rms_g, sc, sh, w_in, ln_g, ln_b, seq):
    m = x2d.shape[0]
    tm = TM_PROJ
    steps_per_seq = seq // tm
    row = lambda i: (i, 0)
    mod = lambda i: (i // steps_per_seq, 0, 0)
    piece = jax.ShapeDtypeStruct((m, BRANCH_WIDTH), BF16)
    return pl.pallas_call(
        _inproj_kernel,
        grid=(m // tm,),
        in_specs=[
            pl.BlockSpec((tm, D_MODEL), row),
            _resident((1, D_MODEL)),
            pl.BlockSpec((1, 1, D_MODEL), mod),
            pl.BlockSpec((1, 1, D_MODEL), mod),
            _resident((D_MODEL, IN_COLS)),
            _resident((1, BRANCH_WIDTH)),
            _resident((1, BRANCH_WIDTH)),
        ],
        out_specs=[pl.BlockSpec((tm, BRANCH_WIDTH), row)] * 7
        + [pl.BlockSpec((tm, N_BRANCH * D_MODEL), row)],
        out_shape=[piece] * 7 + [jax.ShapeDtypeStruct((m, N_BRANCH * D_MODEL), BF16)],
        compiler_params=pltpu.CompilerParams(
            dimension_semantics=("arbitrary",), vmem_limit_bytes=VMEM_LIMIT),
        name="in_proj",
    )(x2d, rms_g, sc, sh, w_in, ln_g, ln_b)


def _mix_kernel(x_ref, u_ref, vln_ref, q0_ref, q1_ref, k_ref, v_ref, p_ref, halo_ref,
                gate0_ref, gate2_ref, gate1_ref, gt_ref,
                ws_ref, bs_ref, pw_ref, ps_ref, wbr_ref, wout_ref, tri_ref, band_ref,
                o_ref, acc_ref, run_ref, br_scr, merged_scr, mergedb_scr):
    i = pl.program_id(1)
    n_tiles = pl.num_programs(1) - 1
    gate_refs = (gate0_ref, gate1_ref, gate2_ref)
    gw = GM_GROUP_WIDTH
    nc = MXU_COLS

    def gmlp_group(g):
        tr = lax.broadcasted_iota(jnp.int32, (GM_CHUNK, GM_CHUNK), 0)
        tc = lax.broadcasted_iota(jnp.int32, (GM_CHUNK, GM_CHUNK), 1)
        cols = slice(g * gw, (g + 1) * gw)
        wg = jnp.where(tc <= tr, ws_ref[g], 0.0).astype(BF16)
        for c in range(TQ // GM_CHUNK):
            rows = slice(c * GM_CHUNK, (c + 1) * GM_CHUNK)
            s = jnp.dot(wg, vln_ref[0, rows, cols], preferred_element_type=F32) + bs_ref[:, cols]
            br_scr[0, rows, cols] = (u_ref[0, rows, cols].astype(F32) * s).astype(BF16)

    def pool_diff(g):
        w = POOL_WINDOWS[g]
        cols = slice(g * POOL_GROUP_WIDTH, (g + 1) * POOL_GROUP_WIDTH)
        history = halo_ref[0, :, cols]
        history = jnp.where(i > 0, history, jnp.zeros_like(history))
        win = jnp.dot(band_ref[g, :, :TQ], p_ref[0, :, cols], preferred_element_type=F32)
        top = jnp.dot(band_ref[g, :POOL_HALO, TQ:], history, preferred_element_type=F32)
        win = jnp.concatenate([win[:POOL_HALO] + top, win[POOL_HALO:]], axis=0)
        cur = p_ref[0, :, cols].astype(F32)
        pos = i * TQ + lax.broadcasted_iota(jnp.int32, (TQ, 1), 0)
        count = jnp.minimum(pos + 1, w).astype(F32)
        br_scr[2, :, cols] = (win / count - cur).astype(BF16)

    def pool_map(g):
        cols = slice(g * POOL_GROUP_WIDTH, (g + 1) * POOL_GROUP_WIDTH)
        y = jnp.dot(br_scr[2, :, cols], pw_ref[g], preferred_element_type=F32) * ps_ref[:, cols]
        br_scr[2, :, cols] = y.astype(BF16)

    def gated_branch(n, cols):
        t = jnp.dot(br_scr[n], wbr_ref[n, :, cols], preferred_element_type=F32)
        return t * gate_refs[n][0, :, cols].astype(F32)

    def merge_local(c):
        cols = slice(c * nc, (c + 1) * nc)
        merged_scr[:, cols] = gated_branch(0, cols) + gated_branch(2, cols)

    def finish_merge(c):
        cols = slice(c * nc, (c + 1) * nc)
        mergedb_scr[:, cols] = (merged_scr[:, cols] + gated_branch(1, cols)).astype(BF16)

    def finish_output(c):
        cols = slice(c * nc, (c + 1) * nc)
        y = jnp.dot(mergedb_scr[...], wout_ref[:, cols], preferred_element_type=F32)
        o_ref[0, :, cols] = x_ref[0, :, cols] + gt_ref[0][:, cols] * y

    chunks = range(D_MODEL // nc)

    def side_work(finish_previous):
        local = []
        for g in range(GM_GROUPS):
            local += [functools.partial(gmlp_group, g), functools.partial(pool_diff, g)]
        maps = [functools.partial(pool_map, g) for g in range(len(POOL_WINDOWS))]
        merges = [functools.partial(merge_local, c) for c in chunks]
        if not finish_previous:
            return local + maps + merges
        return ([functools.partial(finish_merge, c) for c in chunks] + local + maps + merges
                + [functools.partial(finish_output, c) for c in chunks])

    row_id = lax.broadcasted_iota(jnp.int32, (TQ, TK), 0)
    col_id = lax.broadcasted_iota(jnp.int32, (TQ, TK), 1)
    strictly_causal = col_id < row_id

    q_refs = (q0_ref, q1_ref)
    pair_cols = lambda head: slice((head // 2) * HEAD_PAIR, (head // 2 + 1) * HEAD_PAIR)

    def sweep(blocks, skew, side_work=(), heads=tuple(range(SB_HEADS))):
        items = [piece + (head,) for piece in blocks for head in heads]
        side_work = list(side_work)
        n_steps = len(items) + 2 * skew
        side_per_step = -(-len(side_work) // n_steps)
        log_beta, split, sums, weights, av, running = {}, {}, {}, {}, {}, {}

        def scores(n):
            j, row0, nrows, nkeys, masked, first, head = items[n]
            keys = pl.ds(pl.multiple_of(j * TK, TK), nkeys)
            z = lax.dot_general(q_refs[head % 2][0, row0:row0 + nrows, pair_cols(head)],
                                k_ref[0, keys, pair_cols(head)],
                                (((1,), (1,)), ((), ())), preferred_element_type=F32)
            neg_abs = lax.bitcast_convert_type(
                lax.bitcast_convert_type(z, jnp.int32) | jnp.int32(-2 ** 31), F32)
            lse = jnp.log(1.0 + jnp.exp2(neg_abs)) * LOG2E
            log_beta[n] = jnp.minimum(z, 0.0) - lse
            log_rest = log_beta[n] - z
            if masked:
                log_rest = jnp.where(strictly_causal[row0:row0 + nrows, :nkeys], log_rest, 0.0)
            split[n] = log_rest.astype(BF16)
            sums[n] = jnp.broadcast_to(jnp.sum(log_rest, axis=1, keepdims=True), (nrows, LANES))

        def suffix_and_weights(n):
            j, row0, nrows, nkeys, masked, first, head = items[n]
            suffix = jnp.dot(split.pop(n), tri_ref[:nkeys, :nkeys], preferred_element_type=F32)
            total = log_beta.pop(n) + suffix
            if first:
                running[n] = sums.pop(n)
            else:
                later = run_ref[head, row0:row0 + nrows]
                total = total + jnp.concatenate([later] * (nkeys // LANES), axis=1)
                running[n] = later + sums.pop(n)
            run_ref[head, row0:row0 + nrows] = running[n]
            a = jnp.exp2(total)
            if masked:
                a = jnp.where(strictly_causal[row0:row0 + nrows, :nkeys], a, 0.0)
            weights[n] = a.astype(BF16)

        def weighted_values(n):
            j, row0, nrows, nkeys, masked, first, head = items[n]
            keys = pl.ds(pl.multiple_of(j * TK, TK), nkeys)
            av[n] = jnp.dot(weights.pop(n), v_ref[0, keys, pair_cols(head)], preferred_element_type=F32)
            if head % 2:
                first_of_pair = lax.broadcasted_iota(jnp.int32, (nrows, HEAD_PAIR), 1) < SB_HEAD_DIM
                both = jnp.where(first_of_pair, av.pop(n - 1), av.pop(n))
                pair = head // 2
                rows = slice(row0, row0 + nrows)
                acc_ref[pair, rows] = both if first else acc_ref[pair, rows] + both

        for step in range(n_steps):
            if step < len(items):
                scores(step)
            if 0 <= step - skew < len(items):
                suffix_and_weights(step - skew)
            if step >= 2 * skew:
                weighted_values(step - 2 * skew)
            for _ in range(side_per_step):
                if side_work:
                    side_work.pop(0)()
        return [[running[b * len(heads) + h] for h in range(len(heads))] for b in range(len(blocks))]

    def any_live(values):
        least_decayed = functools.reduce(jnp.maximum, values)
        return (jnp.max(least_decayed) > LOG2_F32_UNDERFLOW).astype(jnp.int32)

    half = TQ // 2
    diagonal = [(i, 0, half, half, True, True), (i, half, half, TK, True, True)]

    no_pair_live = (jnp.int32(0),) * N_PAIRS

    def first_tile():
        sweep(diagonal, TILE_SKEW, side_work(False))
        return no_pair_live

    def later_tile():
        run = sweep(diagonal + [(i - 1, 0, TQ, TK, False, False)], TILE_SKEW, side_work(True))
        return tuple(any_live(run[-1][2 * p:2 * p + 2]) for p in range(N_PAIRS))

    def flush_last_tile():
        for c in chunks:
            finish_merge(c)
        for c in chunks:
            finish_output(c)
        return no_pair_live

    live = lax.cond(i == 0, first_tile, lambda: lax.cond(i == n_tiles, flush_last_tile, later_tile))

    for p in range(N_PAIRS):
        def earlier_block(carry, p=p):
            t, _ = carry
            run = sweep([(i - 1 - t, 0, TQ, TK, False, False)], LOOP_SKEW, heads=(2 * p, 2 * p + 1))
            return t + 1, any_live(run[0])

        lax.while_loop(lambda carry: (carry[0] < i) & (carry[1] > 0), earlier_block, (jnp.int32(1), live[p]))

    @pl.when(i < n_tiles)
    def _():
        for p in range(N_PAIRS):
            br_scr[1, :, p * HEAD_PAIR:(p + 1) * HEAD_PAIR] = acc_ref[p].astype(BF16)


def _mix_call(x, u, vln, q0, q1, k, v, p, gates, gt, ws, bs_full, pw, ps, wbr, wout, tri, band):
    batch, seq, _ = x.shape
    bw = BRANCH_WIDTH
    n_tiles = seq // TQ
    tile = lambda s: jnp.minimum(s, n_tiles - 1)
    done = lambda s: jnp.maximum(s - 1, 0)
    rows = lambda b, s: (b, tile(s), 0)
    whole = lambda b, s: (b, 0, 0)
    halo = lambda b, s: (b, jnp.maximum(tile(s) * (TQ // POOL_HALO) - 1, 0), 0)
    piece = pl.BlockSpec((1, TQ, bw), rows)
    gate_of = lambda n, which: pl.BlockSpec((1, TQ, D_MODEL), lambda b, s: (b, which(s), n))
    return pl.pallas_call(
        _mix_kernel,
        grid=(batch, n_tiles + 1),
        in_specs=[
            pl.BlockSpec((1, TQ, D_MODEL), lambda b, s: (b, done(s), 0)),
            piece, piece, piece, piece,
            pl.BlockSpec((1, seq, bw), whole),
            pl.BlockSpec((1, seq, bw), whole),
            piece,
            pl.BlockSpec((1, POOL_HALO, bw), halo),
            gate_of(0, tile), gate_of(2, tile), gate_of(1, done),
            pl.BlockSpec((1, 1, D_MODEL), whole),
            _resident((GM_GROUPS, GM_CHUNK, GM_CHUNK)),
            _resident((GM_CHUNK, bw)),
            _resident((len(POOL_WINDOWS), POOL_GROUP_WIDTH, POOL_GROUP_WIDTH)),
            _resident((1, bw)),
            _resident((N_BRANCH, bw, D_MODEL)),
            _resident((D_MODEL, D_MODEL)),
            _resident((TK, TK)),
            _resident((len(POOL_WINDOWS), TQ, POOL_HALO + TQ)),
        ],
        out_specs=pl.BlockSpec((1, TQ, D_MODEL), lambda b, s: (b, done(s), 0)),
        out_shape=jax.ShapeDtypeStruct(x.shape, F32),
        scratch_shapes=[
            pltpu.VMEM((N_PAIRS, TQ, HEAD_PAIR), F32),
            pltpu.VMEM((SB_HEADS, TQ, LANES), F32),
            pltpu.VMEM((N_BRANCH, TQ, bw), BF16),
            pltpu.VMEM((TQ, D_MODEL), F32),
            pltpu.VMEM((TQ, D_MODEL), BF16),
        ],
        compiler_params=pltpu.CompilerParams(
            dimension_semantics=("arbitrary", "arbitrary"), vmem_limit_bytes=VMEM_LIMIT),
        name="token_mix",
    )(x, u, vln, q0, q1, k, v, p, p, gates, gates, gates, gt, ws, bs_full, pw, ps, wbr, wout, tri, band)


def _ffn_kernel(x_ref, g_ref, sc_ref, sh_ref, gt_ref, win_ref, wout_ref, fg_ref, o_ref, hid_scr, *, final):
    x = x_ref[...]
    h = _rms_modulate(x, g_ref[...], sc_ref[0], sh_ref[0]).astype(BF16)
    for c in range(D_FF // FF_CHUNK):
        c0 = c * FF_CHUNK
        f_gate = jnp.dot(h, win_ref[:, c0:c0 + FF_CHUNK], preferred_element_type=F32)
        f_up = jnp.dot(h, win_ref[:, D_FF + c0:D_FF + c0 + FF_CHUNK], preferred_element_type=F32)
        hid_scr[:, c0:c0 + FF_CHUNK] = (_silu(f_gate) * f_up).astype(BF16)
    y = jnp.dot(hid_scr[...], wout_ref[...], preferred_element_type=F32)
    xn = x + gt_ref[0] * y
    if final:
        ms = jnp.mean(xn * xn, axis=-1, keepdims=True)
        xn = (xn * lax.rsqrt(ms + EPS)) * fg_ref[...]
    o_ref[...] = xn


def _ffn_call(x2d, rms_g, sc, sh, gt, w_in, w_out, final_g, seq, final):
    m = x2d.shape[0]
    tm = TM_PROJ
    steps_per_seq = seq // tm
    row = lambda i: (i, 0)
    mod = lambda i: (i // steps_per_seq, 0, 0)
    return pl.pallas_call(
        functools.partial(_ffn_kernel, final=final),
        grid=(m // tm,),
        in_specs=[
            pl.BlockSpec((tm, D_MODEL), row),
            _resident((1, D_MODEL)),
            pl.BlockSpec((1, 1, D_MODEL), mod),
            pl.BlockSpec((1, 1, D_MODEL), mod),
            pl.BlockSpec((1, 1, D_MODEL), mod),
            _resident((D_MODEL, 2 * D_FF)),
            _resident((D_FF, D_MODEL)),
            _resident((1, D_MODEL)),
        ],
        out_specs=pl.BlockSpec((tm, D_MODEL), row),
        out_shape=jax.ShapeDtypeStruct(x2d.shape, F32),
        scratch_shapes=[pltpu.VMEM((tm, D_FF), BF16)],
        compiler_params=pltpu.CompilerParams(
            dimension_semantics=("arbitrary",), vmem_limit_bytes=VMEM_LIMIT),
        name="ffn_final" if final else "ffn",
    )(x2d, rms_g, sc, sh, gt, w_in, w_out, final_g)


def _suffix_matrix():
    j = lax.broadcasted_iota(jnp.int32, (TK, TK), 0)
    s = lax.broadcasted_iota(jnp.int32, (TK, TK), 1)
    return (j > s).astype(BF16)


def _window_matrices():
    t = lax.broadcasted_iota(jnp.int32, (TQ, TQ + POOL_HALO), 0)
    r = lax.broadcasted_iota(jnp.int32, (TQ, TQ + POOL_HALO), 1)
    back = jnp.where(r < TQ, t - r, t + TQ + POOL_HALO - r)
    return jnp.stack([((back >= 0) & (back < w)).astype(BF16) for w in POOL_WINDOWS])


def kernel(x, c, rms_g1, rms_g2, w_ada, b_ada, w_in, gm_ln_g, gm_ln_b, gm_w_spatial, gm_b_spatial,
           pool_w, pool_scale, w_branch, w_out, w_ffn_in, w_ffn_out, final_g):
    batch, seq, d = x.shape
    assert d == D_MODEL and seq % TM_PROJ == 0 and seq % TQ == 0 and TQ == TK and TQ % GM_CHUNK == 0
    m = batch * seq

    mod = _ada_call(c, w_ada, b_ada)
    mod = mod.reshape(DEPTH, batch, N_MOD, 1, D_MODEL)
    tri = _suffix_matrix()
    band = _window_matrices()
    final_row = final_g.reshape(1, D_MODEL)

    for l in range(DEPTH):
        sh1, sc1, gt1, sh2, sc2, gt2 = [mod[l, :, n] for n in range(N_MOD)]
        u, vln, q0, q1, k, v, p, gates = _inproj_call(
            x.reshape(m, D_MODEL), rms_g1[l].reshape(1, D_MODEL), sc1, sh1, w_in[l].astype(BF16),
            gm_ln_g[l].reshape(1, BRANCH_WIDTH), gm_ln_b[l].reshape(1, BRANCH_WIDTH), seq)
        to_seq = lambda a: a.reshape(batch, seq, a.shape[-1])
        bs_full = jnp.repeat(gm_b_spatial[l].T, GM_GROUP_WIDTH, axis=1)
        x = _mix_call(
            x, to_seq(u), to_seq(vln), to_seq(q0), to_seq(q1), to_seq(k), to_seq(v), to_seq(p),
            to_seq(gates), gt1, gm_w_spatial[l], bs_full, pool_w[l].astype(BF16),
            pool_scale[l].reshape(1, BRANCH_WIDTH), w_branch[l].astype(BF16), w_out[l].astype(BF16), tri,
            band)
        x = _ffn_call(
            x.reshape(m, D_MODEL), rms_g2[l].reshape(1, D_MODEL), sc2, sh2, gt2,
            w_ffn_in[l].astype(BF16), w_ffn_out[l].astype(BF16), final_row, seq,
            final=(l == DEPTH - 1)).reshape(batch, seq, D_MODEL)
    return x
```

```python
import functools

import jax
import jax.numpy as jnp
from jax import lax
from jax.experimental import pallas as pl
from jax.experimental.pallas import tpu as pltpu

F32 = jnp.float32
BF16 = jnp.bfloat16

D_MODEL = 1024
DEPTH = 4
BRANCH_WIDTH = D_MODEL // 2
N_BRANCH = 3
GM_CHUNK = 128
GM_GROUPS = 4
GM_GROUP_WIDTH = BRANCH_WIDTH // GM_GROUPS
SB_HEAD_DIM = 64
SB_HEADS = BRANCH_WIDTH // SB_HEAD_DIM
POOL_WINDOWS = (2, 4, 8, 16)
POOL_GROUP_WIDTH = BRANCH_WIDTH // len(POOL_WINDOWS)
D_FF = -(-8 * D_MODEL // (3 * 256)) * 256
N_MOD = 6
EPS = 1e-6
IN_COLS = 6 * BRANCH_WIDTH + N_BRANCH * D_MODEL

LANES = 128
MXU_COLS = 256
HEAD_PAIR = 2 * SB_HEAD_DIM
N_PAIRS = SB_HEADS // 2
POOL_HALO = 16
LOG2E = 1.4426950408889634
LOG2_F32_UNDERFLOW = -152.0
VMEM_LIMIT = 56 * 1024 * 1024

TM_PROJ = 1024
TQ = 256
TK = 256
TILE_SKEW = 4
LOOP_SKEW = 2
FF_CHUNK = 256


def _resident(shape):
    nd = len(shape)
    return pl.BlockSpec(shape, lambda *_: (0,) * nd, pipeline_mode=pl.Buffered(1))


def _layer_resident(shape, layer):
    nd = len(shape)
    return pl.BlockSpec((None,) + tuple(shape), lambda *_: (layer,) + (0,) * nd, pipeline_mode=pl.Buffered(1))


def _sigmoid(x):
    return 0.5 * jnp.tanh(0.5 * x) + 0.5


def _silu(x):
    half = 0.5 * x
    return half + half * jnp.tanh(half)


def _gelu_tanh(x):
    c = 0.7978845608028654
    half = 0.5 * x
    return half + half * jnp.tanh(x * (c + (c * 0.044715) * (x * x)))


def _rms_modulate(x, g, sc, sh):
    ms = jnp.mean(x * x, axis=-1, keepdims=True)
    return (x * lax.rsqrt(ms + EPS)) * (g * (1.0 + sc)) + sh


def _ada_kernel(c_ref, w_ref, b_ref, o_ref):
    c = c_ref[...]
    c_act = _silu(c)
    o_ref[0] = jnp.dot(c_act.astype(BF16), w_ref[0].astype(BF16), preferred_element_type=F32) + b_ref[0]


def _ada_call(c, w_ada, b_ada):
    batch = c.shape[0]
    n_out = N_MOD * D_MODEL
    tn = n_out // 4
    return pl.pallas_call(
        _ada_kernel,
        grid=(DEPTH, n_out // tn),
        in_specs=[
            pl.BlockSpec((batch, D_MODEL), lambda l, n: (0, 0)),
            pl.BlockSpec((1, D_MODEL, tn), lambda l, n: (l, 0, n)),
            pl.BlockSpec((1, 1, tn), lambda l, n: (l, 0, n)),
        ],
        out_specs=pl.BlockSpec((1, batch, tn), lambda l, n: (l, 0, n)),
        out_shape=jax.ShapeDtypeStruct((DEPTH, batch, n_out), F32),
        compiler_params=pltpu.CompilerParams(
            dimension_semantics=("arbitrary", "arbitrary"), vmem_limit_bytes=VMEM_LIMIT),
        name="ada_mod",
    )(c, w_ada, b_ada.reshape(DEPTH, 1, n_out))


def _inproj_kernel(x_ref, g_ref, sc_ref, sh_ref, w_ref, lng_ref, lnb_ref,
                   u_ref, vln_ref, q0_ref, q1_ref, k_ref, v_ref, p_ref, gate_ref):
    h = _rms_modulate(x_ref[...], g_ref[...], sc_ref[0], sh_ref[0]).astype(BF16)
    bw = BRANCH_WIDTH

    nc = MXU_COLS
    halves = [slice(c * nc, (c + 1) * nc) for c in range(bw // nc)]

    def proj(col0):
        return jnp.dot(h, w_ref[:, col0:col0 + nc], preferred_element_type=F32)

    for c in range(N_BRANCH * D_MODEL // nc):
        gate_ref[:, c * nc:(c + 1) * nc] = _sigmoid(proj(6 * bw + c * nc)).astype(BF16)

    for c, cols in enumerate(halves):
        u_ref[:, cols] = _gelu_tanh(proj(c * nc)).astype(BF16)

    gv = [_gelu_tanh(proj(bw + c * nc)) for c in range(len(halves))]
    mu = sum(jnp.sum(g, axis=-1, keepdims=True) for g in gv) * (1.0 / bw)
    gc = [g - mu for g in gv]
    var = sum(jnp.sum(g * g, axis=-1, keepdims=True) for g in gc) * (1.0 / bw)
    inv = lax.rsqrt(var + EPS)
    for g, cols in zip(gc, halves):
        vln_ref[:, cols] = ((g * inv) * lng_ref[:, cols] + lnb_ref[:, cols]).astype(BF16)

    for n, ref in enumerate((k_ref, v_ref, p_ref)):
        for c, cols in enumerate(halves):
            ref[:, cols] = proj((3 + n) * bw + c * nc).astype(BF16)

    lane = lax.broadcasted_iota(jnp.int32, (1, nc), 1)
    first_of_pair = (lane & SB_HEAD_DIM) == 0
    score_scale = SB_HEAD_DIM ** -0.5 * LOG2E
    scale0 = jnp.where(first_of_pair, score_scale, 0.0)
    scale1 = jnp.where(first_of_pair, 0.0, score_scale)
    for c, cols in enumerate(halves):
        q = proj(2 * bw + c * nc)
        q0_ref[:, cols] = (q * scale0).astype(BF16)
        q1_ref[:, cols] = (q * scale1).astype(BF16)


def _inproj_call(x2d, rms_g, sc, sh, w_in, ln_g, ln_b, seq, layer):
    m = x2d.shape[0]
    tm = TM_PROJ
    steps_per_seq = seq // tm
    row = lambda i: (i, 0)
    mod = lambda i: (i // steps_per_seq, 0, 0)
    piece = jax.ShapeDtypeStruct((m, BRANCH_WIDTH), BF16)
    return pl.pallas_call(
        _inproj_kernel,
        grid=(m // tm,),
        in_specs=[
            pl.BlockSpec((tm, D_MODEL), row),
            _resident((1, D_MODEL)),
            pl.BlockSpec((1, 1, D_MODEL), mod),
            pl.BlockSpec((1, 1, D_MODEL), mod),
            _layer_resident((D_MODEL, IN_COLS), layer),
            _resident((1, BRANCH_WIDTH)),
            _resident((1, BRANCH_WIDTH)),
        ],
        out_specs=[pl.BlockSpec((tm, BRANCH_WIDTH), row)] * 7
        + [pl.BlockSpec((tm, N_BRANCH * D_MODEL), row)],
        out_shape=[piece] * 7 + [jax.ShapeDtypeStruct((m, N_BRANCH * D_MODEL), BF16)],
        compiler_params=pltpu.CompilerParams(
            dimension_semantics=("arbitrary",), vmem_limit_bytes=VMEM_LIMIT),
        name="in_proj",
    )(x2d, rms_g, sc, sh, w_in, ln_g, ln_b)


def _mix_kernel(x_ref, u_ref, vln_ref, q0_ref, q1_ref, k_ref, v_ref, p_ref, halo_ref,
                gate0_ref, gate2_ref, gate1_ref, gt_ref,
                ws_ref, bs_ref, pw_ref, ps_ref, wbr_ref, wout_ref, tri_ref, band_ref,
                o_ref, acc_ref, run_ref, br_scr, merged_scr, mergedb_scr):
    i = pl.program_id(1)
    n_tiles = pl.num_programs(1) - 1
    gate_refs = (gate0_ref, gate1_ref, gate2_ref)
    gw = GM_GROUP_WIDTH
    nc = MXU_COLS

    def gmlp_group(g):
        tr = lax.broadcasted_iota(jnp.int32, (GM_CHUNK, GM_CHUNK), 0)
        tc = lax.broadcasted_iota(jnp.int32, (GM_CHUNK, GM_CHUNK), 1)
        cols = slice(g * gw, (g + 1) * gw)
        wg = jnp.where(tc <= tr, ws_ref[g], 0.0).astype(BF16)
        for c in range(TQ // GM_CHUNK):
            rows = slice(c * GM_CHUNK, (c + 1) * GM_CHUNK)
            s = jnp.dot(wg, vln_ref[0, rows, cols], preferred_element_type=F32) + bs_ref[:, cols]
            br_scr[0, rows, cols] = (u_ref[0, rows, cols].astype(F32) * s).astype(BF16)

    def pool_diff(g):
        w = POOL_WINDOWS[g]
        cols = slice(g * POOL_GROUP_WIDTH, (g + 1) * POOL_GROUP_WIDTH)
        history = halo_ref[0, :, cols]
        history = jnp.where(i > 0, history, jnp.zeros_like(history))
        win = jnp.dot(band_ref[g, :, :TQ], p_ref[0, :, cols], preferred_element_type=F32)
        top = jnp.dot(band_ref[g, :POOL_HALO, TQ:], history, preferred_element_type=F32)
        win = jnp.concatenate([win[:POOL_HALO] + top, win[POOL_HALO:]], axis=0)
        cur = p_ref[0, :, cols].astype(F32)
        pos = i * TQ + lax.broadcasted_iota(jnp.int32, (TQ, 1), 0)
        count = jnp.minimum(pos + 1, w).astype(F32)
        br_scr[2, :, cols] = (win / count - cur).astype(BF16)

    def pool_map(g):
        cols = slice(g * POOL_GROUP_WIDTH, (g + 1) * POOL_GROUP_WIDTH)
        y = jnp.dot(br_scr[2, :, cols], pw_ref[g], preferred_element_type=F32) * ps_ref[:, cols]
        br_scr[2, :, cols] = y.astype(BF16)

    def gated_branch(n, cols):
        t = jnp.dot(br_scr[n], wbr_ref[n, :, cols], preferred_element_type=F32)
        return t * gate_refs[n][0, :, cols].astype(F32)

    def merge_local(c):
        cols = slice(c * nc, (c + 1) * nc)
        merged_scr[:, cols] = gated_branch(0, cols) + gated_branch(2, cols)

    def finish_merge(c):
        cols = slice(c * nc, (c + 1) * nc)
        mergedb_scr[:, cols] = (merged_scr[:, cols] + gated_branch(1, cols)).astype(BF16)

    def finish_output(c):
        cols = slice(c * nc, (c + 1) * nc)
        y = jnp.dot(mergedb_scr[...], wout_ref[:, cols], preferred_element_type=F32)
        o_ref[0, :, cols] = x_ref[0, :, cols] + gt_ref[0][:, cols] * y

    chunks = range(D_MODEL // nc)

    def side_work(finish_previous):
        local = []
        for g in range(GM_GROUPS):
            local += [functools.partial(gmlp_group, g), functools.partial(pool_diff, g)]
        maps = [functools.partial(pool_map, g) for g in range(len(POOL_WINDOWS))]
        merges = [functools.partial(merge_local, c) for c in chunks]
        if not finish_previous:
            return local + maps + merges
        return ([functools.partial(finish_merge, c) for c in chunks] + local + maps + merges
                + [functools.partial(finish_output, c) for c in chunks])

    row_id = lax.broadcasted_iota(jnp.int32, (TQ, TK), 0)
    col_id = lax.broadcasted_iota(jnp.int32, (TQ, TK), 1)
    strictly_causal = col_id < row_id

    q_refs = (q0_ref, q1_ref)
    pair_cols = lambda head: slice((head // 2) * HEAD_PAIR, (head // 2 + 1) * HEAD_PAIR)

    def sweep(blocks, skew, side_work=(), heads=tuple(range(SB_HEADS))):
        items = [piece + (head,) for piece in blocks for head in heads]
        side_work = list(side_work)
        n_steps = len(items) + 2 * skew
        side_per_step = -(-len(side_work) // n_steps)
        log_beta, split, sums, weights, av, running = {}, {}, {}, {}, {}, {}

        def scores(n):
            j, row0, nrows, nkeys, masked, first, head = items[n]
            keys = pl.ds(pl.multiple_of(j * TK, TK), nkeys)
            z = lax.dot_general(q_refs[head % 2][0, row0:row0 + nrows, pair_cols(head)],
                                k_ref[0, keys, pair_cols(head)],
                                (((1,), (1,)), ((), ())), preferred_element_type=F32)
            neg_abs = lax.bitcast_convert_type(
                lax.bitcast_convert_type(z, jnp.int32) | jnp.int32(-2 ** 31), F32)
            lse = jnp.log(1.0 + jnp.exp2(neg_abs)) * LOG2E
            log_beta[n] = jnp.minimum(z, 0.0) - lse
            log_rest = log_beta[n] - z
            if masked:
                log_rest = jnp.where(strictly_causal[row0:row0 + nrows, :nkeys], log_rest, 0.0)
            split[n] = log_rest.astype(BF16)
            sums[n] = jnp.broadcast_to(jnp.sum(log_rest, axis=1, keepdims=True), (nrows, LANES))

        def suffix_and_weights(n):
            j, row0, nrows, nkeys, masked, first, head = items[n]
            suffix = jnp.dot(split.pop(n), tri_ref[:nkeys, :nkeys], preferred_element_type=F32)
            total = log_beta.pop(n) + suffix
            if first:
                running[n] = sums.pop(n)
            else:
                later = run_ref[head, row0:row0 + nrows]
                total = total + jnp.concatenate([later] * (nkeys // LANES), axis=1)
                running[n] = later + sums.pop(n)
            run_ref[head, row0:row0 + nrows] = running[n]
            a = jnp.exp2(total)
            if masked:
                a = jnp.where(strictly_causal[row0:row0 + nrows, :nkeys], a, 0.0)
            weights[n] = a.astype(BF16)

        def weighted_values(n):
            j, row0, nrows, nkeys, masked, first, head = items[n]
            keys = pl.ds(pl.multiple_of(j * TK, TK), nkeys)
            av[n] = jnp.dot(weights.pop(n), v_ref[0, keys, pair_cols(head)], preferred_element_type=F32)
            if head % 2:
                first_of_pair = lax.broadcasted_iota(jnp.int32, (nrows, HEAD_PAIR), 1) < SB_HEAD_DIM
                both = jnp.where(first_of_pair, av.pop(n - 1), av.pop(n))
                pair = head // 2
                rows = slice(row0, row0 + nrows)
                acc_ref[pair, rows] = both if first else acc_ref[pair, rows] + both

        for step in range(n_steps):
            if step < len(items):
                scores(step)
            if 0 <= step - skew < len(items):
                suffix_and_weights(step - skew)
            if step >= 2 * skew:
                weighted_values(step - 2 * skew)
            for _ in range(side_per_step):
                if side_work:
                    side_work.pop(0)()
        return [[running[b * len(heads) + h] for h in range(len(heads))] for b in range(len(blocks))]

    def any_live(values):
        least_decayed = functools.reduce(jnp.maximum, values)
        return (jnp.max(least_decayed) > LOG2_F32_UNDERFLOW).astype(jnp.int32)

    half = TQ // 2
    diagonal = [(i, 0, half, half, True, True), (i, half, half, TK, True, True)]

    no_pair_live = (jnp.int32(0),) * N_PAIRS

    def first_tile():
        sweep(diagonal, TILE_SKEW, side_work(False))
        return no_pair_live

    def later_tile():
        run = sweep(diagonal + [(i - 1, 0, TQ, TK, False, False)], TILE_SKEW, side_work(True))
        return tuple(any_live(run[-1][2 * p:2 * p + 2]) for p in range(N_PAIRS))

    def flush_last_tile():
        for c in chunks:
            finish_merge(c)
        for c in chunks:
            finish_output(c)
        return no_pair_live

    live = lax.cond(i == 0, first_tile, lambda: lax.cond(i == n_tiles, flush_last_tile, later_tile))

    for p in range(N_PAIRS):
        def earlier_block(carry, p=p):
            t, _ = carry
            run = sweep([(i - 1 - t, 0, TQ, TK, False, False)], LOOP_SKEW, heads=(2 * p, 2 * p + 1))
            return t + 1, any_live(run[0])

        lax.while_loop(lambda carry: (carry[0] < i) & (carry[1] > 0), earlier_block, (jnp.int32(1), live[p]))

    @pl.when(i < n_tiles)
    def _():
        for p in range(N_PAIRS):
            br_scr[1, :, p * HEAD_PAIR:(p + 1) * HEAD_PAIR] = acc_ref[p].astype(BF16)


def _mix_call(x, u, vln, q0, q1, k, v, p, gates, gt, ws, bs_full, pw, ps, wbr, wout, tri, band, layer):
    batch, seq, _ = x.shape
    bw = BRANCH_WIDTH
    n_tiles = seq // TQ
    tile = lambda s: jnp.minimum(s, n_tiles - 1)
    done = lambda s: jnp.maximum(s - 1, 0)
    rows = lambda b, s: (b, tile(s), 0)
    whole = lambda b, s: (b, 0, 0)
    halo = lambda b, s: (b, jnp.maximum(tile(s) * (TQ // POOL_HALO) - 1, 0), 0)
    piece = pl.BlockSpec((1, TQ, bw), rows)
    gate_of = lambda n, which: pl.BlockSpec((1, TQ, D_MODEL), lambda b, s: (b, which(s), n))
    return pl.pallas_call(
        _mix_kernel,
        grid=(batch, n_tiles + 1),
        in_specs=[
            pl.BlockSpec((1, TQ, D_MODEL), lambda b, s: (b, done(s), 0)),
            piece, piece, piece, piece,
            pl.BlockSpec((1, seq, bw), whole),
            pl.BlockSpec((1, seq, bw), whole),
            piece,
            pl.BlockSpec((1, POOL_HALO, bw), halo),
            gate_of(0, tile), gate_of(2, tile), gate_of(1, done),
            pl.BlockSpec((1, 1, D_MODEL), whole),
            _resident((GM_GROUPS, GM_CHUNK, GM_CHUNK)),
            _resident((GM_CHUNK, bw)),
            _resident((len(POOL_WINDOWS), POOL_GROUP_WIDTH, POOL_GROUP_WIDTH)),
            _resident((1, bw)),
            _layer_resident((N_BRANCH, bw, D_MODEL), layer),
            _layer_resident((D_MODEL, D_MODEL), layer),
            _resident((TK, TK)),
            _resident((len(POOL_WINDOWS), TQ, POOL_HALO + TQ)),
        ],
        out_specs=pl.BlockSpec((1, TQ, D_MODEL), lambda b, s: (b, done(s), 0)),
        out_shape=jax.ShapeDtypeStruct(x.shape, F32),
        scratch_shapes=[
            pltpu.VMEM((N_PAIRS, TQ, HEAD_PAIR), F32),
            pltpu.VMEM((SB_HEADS, TQ, LANES), F32),
            pltpu.VMEM((N_BRANCH, TQ, bw), BF16),
            pltpu.VMEM((TQ, D_MODEL), F32),
            pltpu.VMEM((TQ, D_MODEL), BF16),
        ],
        compiler_params=pltpu.CompilerParams(
            dimension_semantics=("arbitrary", "arbitrary"), vmem_limit_bytes=VMEM_LIMIT),
        name="token_mix",
    )(x, u, vln, q0, q1, k, v, p, p, gates, gates, gates, gt, ws, bs_full, pw, ps, wbr, wout, tri, band)


def _ffn_kernel(x_ref, g_ref, sc_ref, sh_ref, gt_ref, win_ref, wout_ref, fg_ref, o_ref, hid_scr, *, final):
    x = x_ref[...]
    h = _rms_modulate(x, g_ref[...], sc_ref[0], sh_ref[0]).astype(BF16)
    for c in range(D_FF // FF_CHUNK):
        c0 = c * FF_CHUNK
        f_gate = jnp.dot(h, win_ref[:, c0:c0 + FF_CHUNK], preferred_element_type=F32)
        f_up = jnp.dot(h, win_ref[:, D_FF + c0:D_FF + c0 + FF_CHUNK], preferred_element_type=F32)
        hid_scr[:, c0:c0 + FF_CHUNK] = (_silu(f_gate) * f_up).astype(BF16)
    y = jnp.dot(hid_scr[...], wout_ref[...], preferred_element_type=F32)
    xn = x + gt_ref[0] * y
    if final:
        ms = jnp.mean(xn * xn, axis=-1, keepdims=True)
        xn = (xn * lax.rsqrt(ms + EPS)) * fg_ref[...]
    o_ref[...] = xn


def _ffn_call(x2d, rms_g, sc, sh, gt, w_in, w_out, final_g, seq, layer):
    final = layer == DEPTH - 1
    m = x2d.shape[0]
    tm = TM_PROJ
    steps_per_seq = seq // tm
    row = lambda i: (i, 0)
    mod = lambda i: (i // steps_per_seq, 0, 0)
    return pl.pallas_call(
        functools.partial(_ffn_kernel, final=final),
        grid=(m // tm,),
        in_specs=[
            pl.BlockSpec((tm, D_MODEL), row),
            _resident((1, D_MODEL)),
            pl.BlockSpec((1, 1, D_MODEL), mod),
            pl.BlockSpec((1, 1, D_MODEL), mod),
            pl.BlockSpec((1, 1, D_MODEL), mod),
            _layer_resident((D_MODEL, 2 * D_FF), layer),
            _layer_resident((D_FF, D_MODEL), layer),
            _resident((1, D_MODEL)),
        ],
        out_specs=pl.BlockSpec((tm, D_MODEL), row),
        out_shape=jax.ShapeDtypeStruct(x2d.shape, F32),
        scratch_shapes=[pltpu.VMEM((tm, D_FF), BF16)],
        compiler_params=pltpu.CompilerParams(
            dimension_semantics=("arbitrary",), vmem_limit_bytes=VMEM_LIMIT),
        name="ffn_final" if final else "ffn",
    )(x2d, rms_g, sc, sh, gt, w_in, w_out, final_g)


def _suffix_matrix():
    j = lax.broadcasted_iota(jnp.int32, (TK, TK), 0)
    s = lax.broadcasted_iota(jnp.int32, (TK, TK), 1)
    return (j > s).astype(BF16)


def _window_matrices():
    t = lax.broadcasted_iota(jnp.int32, (TQ, TQ + POOL_HALO), 0)
    r = lax.broadcasted_iota(jnp.int32, (TQ, TQ + POOL_HALO), 1)
    back = jnp.where(r < TQ, t - r, t + TQ + POOL_HALO - r)
    return jnp.stack([((back >= 0) & (back < w)).astype(BF16) for w in POOL_WINDOWS])


def kernel(x, c, rms_g1, rms_g2, w_ada, b_ada, w_in, gm_ln_g, gm_ln_b, gm_w_spatial, gm_b_spatial,
           pool_w, pool_scale, w_branch, w_out, w_ffn_in, w_ffn_out, final_g):
    batch, seq, d = x.shape
    assert d == D_MODEL and seq % TM_PROJ == 0 and seq % TQ == 0 and TQ == TK and TQ % GM_CHUNK == 0
    m = batch * seq

    mod = _ada_call(c, w_ada, b_ada)
    mod = mod.reshape(DEPTH, batch, N_MOD, 1, D_MODEL)
    tri = _suffix_matrix()
    band = _window_matrices()
    final_row = final_g.reshape(1, D_MODEL)
    w_in, w_branch, w_out, w_ffn_in, w_ffn_out = (
        w.astype(BF16) for w in (w_in, w_branch, w_out, w_ffn_in, w_ffn_out))

    for l in range(DEPTH):
        sh1, sc1, gt1, sh2, sc2, gt2 = [mod[l, :, n] for n in range(N_MOD)]
        u, vln, q0, q1, k, v, p, gates = _inproj_call(
            x.reshape(m, D_MODEL), rms_g1[l].reshape(1, D_MODEL), sc1, sh1, w_in,
            gm_ln_g[l].reshape(1, BRANCH_WIDTH), gm_ln_b[l].reshape(1, BRANCH_WIDTH), seq, l)
        to_seq = lambda a: a.reshape(batch, seq, a.shape[-1])
        bs_full = jnp.repeat(gm_b_spatial[l].T, GM_GROUP_WIDTH, axis=1)
        x = _mix_call(
            x, to_seq(u), to_seq(vln), to_seq(q0), to_seq(q1), to_seq(k), to_seq(v), to_seq(p),
            to_seq(gates), gt1, gm_w_spatial[l], bs_full, pool_w[l].astype(BF16),
            pool_scale[l].reshape(1, BRANCH_WIDTH), w_branch, w_out, tri, band, l)
        x = _ffn_call(
            x.reshape(m, D_MODEL), rms_g2[l].reshape(1, D_MODEL), sc2, sh2, gt2,
            w_ffn_in, w_ffn_out, final_row, seq, l).reshape(batch, seq, D_MODEL)
    return x
```
